```python
import math, functools
import jax, jax.numpy as jnp
from jax import lax
import numpy as np

D_MODEL = 1024
BATCH = 32
SEQ = 2048
DEPTH = 2
DEC_BATCH = 128
DEC_SEQ = 4
PAST_LEN = 16384
PAGE_SIZE = 128

L_EVEN = (DEPTH + 1) // 2
L_ODD = DEPTH // 2
A_WIDTH = D_MODEL // 2
A_GROUP = 16
A_GROUPS = A_WIDTH // A_GROUP
A_STATE = 64
B_HEADS = 4
B_DK = 64
B_DV = 2 * B_DK
B_ROT = B_DK // 4
B_QK = B_HEADS * 2 * B_DK
EVEN_IN = A_WIDTH + 2 * B_QK + B_HEADS * B_DV
EVEN_OUT = A_WIDTH + B_HEADS * B_DV
C_HEADS = 4
C_DK = 128
C_DV = 128
C_CONV = 4
C_CHUNK = 64
C_QK = C_HEADS * C_DK
C_V = C_HEADS * C_DV
C_CONV_CH = 2 * C_QK + C_V
C_IN = C_CONV_CH + C_V + 2 * C_HEADS
D_HEADS = 4
D_NOPE = 128
D_ROPE = 32
D_DV = 128
D_Q_LORA = 384
D_KV_LORA = 256
D_IN = D_Q_LORA + D_KV_LORA + D_ROPE
ODD_IN = C_IN + D_IN
ODD_OUT = C_V + D_HEADS * D_DV
ROPE_THETA = 500000.0
MLA_ROPE_THETA = 10000.0
ATTN_BLOCK = 128
N_EXPERTS = 32
TOP_K = 4
D_FF = 1024
SWIGLU_LIMIT = 7.0
SWIGLU_ALPHA = 1.702
MOE_BLOCK = 128
DN_ALPHA = (2 * DEPTH) ** 0.25
DN_BETA = (8 * DEPTH) ** -0.25
EPS = 1e-6
NEG_BIG = -1e30

kernel_name = 'hybrid_s5_diffattn_gdn_mla_moe_step'


def _rms_norm(x, g):
    xf = x.astype(jnp.float32)
    y = xf * lax.rsqrt(jnp.mean(xf * xf, axis=-1, keepdims=True) + EPS)
    return (y * g.astype(jnp.float32)).astype(x.dtype)


def _layer_norm(x, g, b):
    xf = x.astype(jnp.float32)
    mu = jnp.mean(xf, axis=-1, keepdims=True)
    var = jnp.mean(jnp.square(xf - mu), axis=-1, keepdims=True)
    y = (xf - mu) * lax.rsqrt(var + EPS) * g.astype(jnp.float32) + b.astype(jnp.float32)
    return y.astype(x.dtype)


def _rope(x, pos, theta, n_rot):
    half = n_rot // 2
    inv = jnp.exp(-math.log(theta) * jnp.arange(half, dtype=jnp.float32) * (2.0 / n_rot))
    ang = pos[:, None] * inv[None, :]
    cos = jnp.cos(ang)[:, None, :]
    sin = jnp.sin(ang)[:, None, :]
    xf = x.astype(jnp.float32)
    x1 = xf[..., :half]
    x2 = xf[..., half:n_rot]
    out = jnp.concatenate([x1 * cos - x2 * sin, x2 * cos + x1 * sin, xf[..., n_rot:]], axis=-1)
    return out.astype(x.dtype)


def _online_softmax_step(carry, s, vals, pv):
    m, l, acc = carry
    m_new = jnp.maximum(m, jnp.max(s, axis=-1))
    corr = jnp.exp(m - m_new)
    p = jnp.exp(s - m_new[..., None])
    return (m_new, l * corr + jnp.sum(p, axis=-1), acc * corr[..., None] + pv(p, vals))


def _s5_scan(u, h0_re, h0_im, lam_re, lam_im, log_dt, b_re, b_im, c_re, c_im, d_skip):
    bsz, L, _ = u.shape
    uf = u.astype(jnp.float32).reshape(bsz, L, A_GROUPS, A_GROUP)
    lr = lam_re.astype(jnp.float32)
    li = lam_im.astype(jnp.float32)
    dt = jnp.exp(log_dt.astype(jnp.float32))[:, None]
    mag = jnp.exp(lr * dt)
    ar = mag * jnp.cos(li * dt)
    ai = mag * jnp.sin(li * dt)
    den = lr * lr + li * li
    fr = ((ar - 1.0) * lr + ai * li) / den
    fi = (ai * lr - (ar - 1.0) * li) / den
    br = b_re.astype(jnp.float32)
    bi = b_im.astype(jnp.float32)
    bbr = fr[..., None] * br - fi[..., None] * bi
    bbi = fr[..., None] * bi + fi[..., None] * br
    xr = jnp.einsum('blgp,gnp->blgn', uf, bbr)
    xi = jnp.einsum('blgp,gnp->blgn', uf, bbi)
    h0r = h0_re.astype(jnp.float32)
    h0i = h0_im.astype(jnp.float32)
    xr = xr.at[:, 0].add(ar * h0r - ai * h0i)
    xi = xi.at[:, 0].add(ar * h0i + ai * h0r)
    a_r = jnp.broadcast_to(ar, xr.shape)
    a_i = jnp.broadcast_to(ai, xi.shape)

    def combine(e1, e2):
        a1r, a1i, b1r, b1i = e1
        a2r, a2i, b2r, b2i = e2
        return (a2r * a1r - a2i * a1i, a2r * a1i + a2i * a1r,
                a2r * b1r - a2i * b1i + b2r, a2r * b1i + a2i * b1r + b2i)

    _, _, hr, hi = lax.associative_scan(combine, (a_r, a_i, xr, xi), axis=1)
    y = (jnp.einsum('gpn,blgn->blgp', c_re.astype(jnp.float32), hr)
         - jnp.einsum('gpn,blgn->blgp', c_im.astype(jnp.float32), hi)
         + d_skip.astype(jnp.float32) * uf)
    return y.reshape(bsz, L, A_WIDTH).astype(u.dtype), hr[:, -1], hi[:, -1]


def _diff_attn_prompt(q, k, v):
    bsz, L = q.shape[:2]
    nb = L // ATTN_BLOCK
    qb = q.reshape(bsz, nb, ATTN_BLOCK, B_HEADS, 2, B_DK).swapaxes(0, 1)
    kpos = jnp.arange(L)
    scale = B_DK ** -0.5

    def block(args):
        i, qi = args
        s = jnp.einsum('bqhcd,bkhcd->bhcqk', qi, k, preferred_element_type=jnp.float32) * scale
        qpos = i * ATTN_BLOCK + jnp.arange(ATTN_BLOCK)
        s = jnp.where(kpos[None, :] <= qpos[:, None], s, -jnp.inf)
        p = jax.nn.softmax(s, axis=-1)
        return jnp.einsum('bhcqk,bkhd->bqhcd', p.astype(v.dtype), v)

    o = lax.map(block, (jnp.arange(nb), qb))
    return o.swapaxes(0, 1).reshape(bsz, L, B_HEADS, 2, B_DV)


def _diff_attn_sample(q, k, v, cache_k, cache_v, layer, page_table):
    bd, T = q.shape[:2]
    scale = B_DK ** -0.5

    def pv(p, vals):
        return jnp.einsum('bhcts,bshd->bhctd', p, vals.astype(jnp.float32))

    def page_step(carry, pages):
        kp = cache_k[layer, pages].reshape(bd, PAGE_SIZE, B_HEADS, 2, B_DK)
        vp = cache_v[layer, pages]
        s = jnp.einsum('bthcd,bshcd->bhcts', q, kp, preferred_element_type=jnp.float32) * scale
        return _online_softmax_step(carry, s, vp, pv), None

    init = (jnp.full((bd, B_HEADS, 2, T), NEG_BIG, jnp.float32),
            jnp.zeros((bd, B_HEADS, 2, T), jnp.float32),
            jnp.zeros((bd, B_HEADS, 2, T, B_DV), jnp.float32))
    carry, _ = lax.scan(page_step, init, page_table.T)
    tq = jnp.arange(T)
    s_self = jnp.einsum('bthcd,bshcd->bhcts', q, k, preferred_element_type=jnp.float32) * scale
    s_self = jnp.where(tq[None, :] <= tq[:, None], s_self, -jnp.inf)
    _, l, acc = _online_softmax_step(carry, s_self, v, pv)
    o = acc / l[..., None]
    return o.transpose(0, 3, 1, 2, 4)


def _even_mixer(x, pos, h0_re, h0_im, attend, lam_init, w_in, lam_re, lam_im, log_dt, b_re, b_im,
                c_re, c_im, d_skip, w_glu, lq1, lk1, lq2, lk2, subln, w_out):
    bsz, L, _ = x.shape
    proj = x @ w_in
    u = proj[..., :A_WIDTH]
    q = proj[..., A_WIDTH:A_WIDTH + B_QK]
    k = proj[..., A_WIDTH + B_QK:A_WIDTH + 2 * B_QK]
    v = proj[..., A_WIDTH + 2 * B_QK:].reshape(bsz, L, B_HEADS, B_DV)
    y_a, h_re, h_im = _s5_scan(u, h0_re, h0_im, lam_re, lam_im, log_dt, b_re, b_im, c_re, c_im, d_skip)
    y_a = jax.nn.gelu(y_a)
    y_a = y_a * jax.nn.sigmoid(y_a @ w_glu)
    q = _rope(q.reshape(bsz, L, 2 * B_HEADS, B_DK), pos, ROPE_THETA, B_ROT).reshape(bsz, L, B_HEADS, 2, B_DK)
    k = _rope(k.reshape(bsz, L, 2 * B_HEADS, B_DK), pos, ROPE_THETA, B_ROT).reshape(bsz, L, B_HEADS, 2, B_DK)
    o = attend(q, k, v)
    lam = (jnp.exp(jnp.sum(lq1.astype(jnp.float32) * lk1.astype(jnp.float32)))
           - jnp.exp(jnp.sum(lq2.astype(jnp.float32) * lk2.astype(jnp.float32))) + lam_init)
    y_b = o[..., 0, :].astype(jnp.float32) - lam * o[..., 1, :].astype(jnp.float32)
    y_b = _rms_norm(y_b, subln) * (1.0 - lam_init)
    y_b = y_b.reshape(bsz, L, B_HEADS * B_DV).astype(x.dtype)
    y = jnp.concatenate([y_a, y_b], axis=-1) @ w_out
    return y, h_re, h_im, k.reshape(bsz, L, B_HEADS, 2 * B_DK), v


def _gated_delta_chunked(q, k, v, beta, g, s0):
    bsz, L, H, _ = q.shape
    C = min(C_CHUNK, L)
    nc = -(-L // C)
    pad = nc * C - L

    def blocks(t):
        t = jnp.pad(t, [(0, 0), (0, pad)] + [(0, 0)] * (t.ndim - 2))
        t = t.reshape((bsz, nc, C) + t.shape[2:])
        return jnp.moveaxis(t, 3, 2).swapaxes(0, 1)

    qc, kc, vc, bc, gc = blocks(q), blocks(k), blocks(v), blocks(beta), blocks(g)
    g_cum = jnp.cumsum(gc, axis=-1)
    idx = jnp.arange(C)
    causal = idx[:, None] >= idx[None, :]
    strict = idx[:, None] > idx[None, :]
    decay = jnp.exp(jnp.where(causal, g_cum[..., :, None] - g_cum[..., None, :], -jnp.inf))
    k_beta = kc * bc[..., None]
    v_beta = vc * bc[..., None]
    m = jnp.where(strict, jnp.einsum('nbhik,nbhjk->nbhij', k_beta, kc) * decay, 0.0)
    rhs = jnp.concatenate([v_beta, k_beta * jnp.exp(g_cum)[..., None]], axis=-1)
    sol = lax.linalg.triangular_solve(m, rhs, left_side=True, lower=True, unit_diagonal=True)
    u_c = sol[..., :C_DV]
    w_c = sol[..., C_DV:]
    a_intra = jnp.einsum('nbhik,nbhjk->nbhij', qc, kc) * decay

    def step(S, xs):
        q_i, k_i, u_i, w_i, g_i, a_i = xs
        v_new = u_i - jnp.einsum('bhck,bhkv->bhcv', w_i, S)
        o_i = (jnp.einsum('bhck,bhkv->bhcv', q_i * jnp.exp(g_i)[..., None], S)
               + jnp.einsum('bhij,bhjv->bhiv', a_i, v_new))
        g_last = g_i[..., -1:]
        S = (S * jnp.exp(g_last)[..., None]
             + jnp.einsum('bhck,bhcv->bhkv', k_i * jnp.exp(g_last - g_i)[..., None], v_new))
        return S, o_i

    S, o = lax.scan(step, s0, (qc, kc, u_c, w_c, g_cum, a_intra))
    o = jnp.moveaxis(o.swapaxes(0, 1), 2, 3).reshape(bsz, nc * C, H, C_DV)[:, :L]
    return o, S


def _gdn_mixer(p, conv0, s0, conv_w, a_log, dt_bias, norm_g):
    bsz, L, _ = p.shape
    qkv = p[..., :C_CONV_CH]
    z = p[..., C_CONV_CH:C_CONV_CH + C_V]
    b = p[..., C_CONV_CH + C_V:C_CONV_CH + C_V + C_HEADS]
    a = p[..., C_CONV_CH + C_V + C_HEADS:]
    xp = jnp.concatenate([conv0.astype(p.dtype), qkv], axis=1)
    conv = sum(conv_w[j] * xp[:, j:j + L] for j in range(C_CONV))
    new_buf = xp[:, xp.shape[1] - (C_CONV - 1):]
    act = jax.nn.silu(conv).astype(jnp.float32)
    q = act[..., :C_QK].reshape(bsz, L, C_HEADS, C_DK)
    k = act[..., C_QK:2 * C_QK].reshape(bsz, L, C_HEADS, C_DK)
    v = act[..., 2 * C_QK:].reshape(bsz, L, C_HEADS, C_DV)
    q = q * lax.rsqrt(jnp.sum(q * q, axis=-1, keepdims=True) + EPS) * (C_DK ** -0.5)
    k = k * lax.rsqrt(jnp.sum(k * k, axis=-1, keepdims=True) + EPS)
    beta = jax.nn.sigmoid(b.astype(jnp.float32))
    g = -jnp.exp(a_log.astype(jnp.float32)) * jax.nn.softplus(a.astype(jnp.float32) + dt_bias.astype(jnp.float32))
    o, S = _gated_delta_chunked(q, k, v, beta, g, s0.astype(jnp.float32))
    o = _rms_norm(o, norm_g) * jax.nn.silu(z.astype(jnp.float32).reshape(bsz, L, C_HEADS, C_DV))
    return o.reshape(bsz, L, C_V).astype(p.dtype), S, new_buf


def _mla_project(p, pos, q_norm, w_uq, kv_norm, w_uk):
    bsz, L, _ = p.shape
    c_q = p[..., :D_Q_LORA]
    c_kv = p[..., D_Q_LORA:D_Q_LORA + D_KV_LORA]
    k_r = p[..., D_Q_LORA + D_KV_LORA:]
    q = (_rms_norm(c_q, q_norm) @ w_uq).reshape(bsz, L, D_HEADS, D_NOPE + D_ROPE)
    q_rope = _rope(q[..., D_NOPE:], pos, MLA_ROPE_THETA, D_ROPE)
    q_lat = jnp.einsum('blhd,hdr->blhr', q[..., :D_NOPE], w_uk)
    c = _rms_norm(c_kv, kv_norm)
    kr = _rope(k_r[:, :, None, :], pos, MLA_ROPE_THETA, D_ROPE)[:, :, 0, :]
    return q_lat, q_rope, c, kr


def _mla_attn_prompt(q_lat, q_rope, c, kr):
    bsz, L = q_lat.shape[:2]
    nb = L // ATTN_BLOCK
    ql = q_lat.reshape(bsz, nb, ATTN_BLOCK, D_HEADS, D_KV_LORA).swapaxes(0, 1)
    qr = q_rope.reshape(bsz, nb, ATTN_BLOCK, D_HEADS, D_ROPE).swapaxes(0, 1)
    kpos = jnp.arange(L)
    scale = (D_NOPE + D_ROPE) ** -0.5

    def block(args):
        i, qli, qri = args
        s = (jnp.einsum('bqhr,bkr->bhqk', qli, c, preferred_element_type=jnp.float32)
             + jnp.einsum('bqhd,bkd->bhqk', qri, kr, preferred_element_type=jnp.float32)) * scale
        qpos = i * ATTN_BLOCK + jnp.arange(ATTN_BLOCK)
        s = jnp.where(kpos[None, :] <= qpos[:, None], s, -jnp.inf)
        p = jax.nn.softmax(s, axis=-1)
        return jnp.einsum('bhqk,bkr->bqhr', p.astype(c.dtype), c)

    o = lax.map(block, (jnp.arange(nb), ql, qr))
    return o.swapaxes(0, 1).reshape(bsz, L, D_HEADS, D_KV_LORA)


def _mla_attn_sample(q_lat, q_rope, c, kr, cache_lat, cache_rope, layer, page_table):
    bd, T = q_lat.shape[:2]
    scale = (D_NOPE + D_ROPE) ** -0.5

    def pv(p, vals):
        return jnp.einsum('bhts,bsr->bhtr', p, vals.astype(jnp.float32))

    def scores(cc, rr):
        return (jnp.einsum('bthr,bsr->bhts', q_lat, cc, preferred_element_type=jnp.float32)
                + jnp.einsum('bthd,bsd->bhts', q_rope, rr, preferred_element_type=jnp.float32)) * scale

    def page_step(carry, pages):
        cc = cache_lat[layer, pages]
        rr = cache_rope[layer, pages]
        return _online_softmax_step(carry, scores(cc, rr), cc, pv), None

    init = (jnp.full((bd, D_HEADS, T), NEG_BIG, jnp.float32),
            jnp.zeros((bd, D_HEADS, T), jnp.float32),
            jnp.zeros((bd, D_HEADS, T, D_KV_LORA), jnp.float32))
    carry, _ = lax.scan(page_step, init, page_table.T)
    tq = jnp.arange(T)
    s_self = jnp.where(tq[None, :] <= tq[:, None], scores(c, kr), -jnp.inf)
    _, l, acc = _online_softmax_step(carry, s_self, c, pv)
    return (acc / l[..., None]).transpose(0, 2, 1, 3)


def _odd_mixer(x, pos, conv0, s0, attend, w_in, conv_w, a_log, dt_bias, gdn_g,
               q_norm, w_uq, kv_norm, w_uk, w_uv, w_out):
    bsz, L, _ = x.shape
    proj = x @ w_in
    y_c, S, conv_new = _gdn_mixer(proj[..., :C_IN], conv0, s0, conv_w, a_log, dt_bias, gdn_g)
    q_lat, q_rope, c, kr = _mla_project(proj[..., C_IN:], pos, q_norm, w_uq, kv_norm, w_uk)
    o_lat = attend(q_lat, q_rope, c, kr).astype(x.dtype)
    y_d = jnp.einsum('blhr,hrd->blhd', o_lat, w_uv).reshape(bsz, L, D_HEADS * D_DV)
    y = jnp.concatenate([y_c, y_d], axis=-1) @ w_out
    return y, S, conv_new, c, kr


def _moe(x, router_w, router_b, w_gu, b_gu, w_down, b_down):
    bsz, L, D = x.shape
    xt = x.reshape(-1, D)
    N = xt.shape[0]
    logits = (xt @ router_w).astype(jnp.float32) + router_b.astype(jnp.float32)
    top_val, top_idx = lax.top_k(logits, TOP_K)
    gates = jax.nn.softmax(top_val, axis=-1)
    NK = N * TOP_K
    flat_e = top_idx.reshape(-1).astype(jnp.int32)
    flat_tok = jnp.arange(NK, dtype=jnp.int32) // TOP_K
    order = jnp.argsort(flat_e)
    sorted_e = flat_e[order]
    counts = jnp.zeros((N_EXPERTS,), jnp.int32).at[flat_e].add(1)
    padded = (counts + MOE_BLOCK - 1) // MOE_BLOCK * MOE_BLOCK
    start = jnp.cumsum(counts) - counts
    pend = jnp.cumsum(padded)
    pstart = pend - padded
    dest = pstart[sorted_e] + (jnp.arange(NK, dtype=jnp.int32) - start[sorted_e])
    n_blocks = -(-(NK + N_EXPERTS * (MOE_BLOCK - 1)) // MOE_BLOCK)
    P = n_blocks * MOE_BLOCK
    buf_tok = jnp.full((P,), N, jnp.int32).at[dest].set(flat_tok[order])
    block_e = jnp.minimum(jnp.searchsorted(pend, jnp.arange(n_blocks, dtype=jnp.int32) * MOE_BLOCK, side='right'),
                          N_EXPERTS - 1)
    x_pad = jnp.concatenate([xt, jnp.zeros((1, D), xt.dtype)], axis=0)

    def expert_block(args):
        toks, ei = args
        h = x_pad[toks] @ w_gu[ei] + b_gu[ei]
        gate = jnp.minimum(h[:, :D_FF], SWIGLU_LIMIT)
        up = jnp.clip(h[:, D_FF:], -SWIGLU_LIMIT, SWIGLU_LIMIT)
        act = (up + 1.0) * gate * jax.nn.sigmoid(SWIGLU_ALPHA * gate)
        return act @ w_down[ei] + b_down[ei]

    y_buf = lax.map(expert_block, (buf_tok.reshape(n_blocks, MOE_BLOCK), block_e))
    y_sorted = y_buf.reshape(P, D)[dest]
    y_assign = jnp.zeros((NK, D), y_sorted.dtype).at[order].set(y_sorted)
    y = jnp.einsum('nkd,nk->nd', y_assign.reshape(N, TOP_K, D), gates.astype(y_sorted.dtype))
    return y.reshape(bsz, L, D).astype(x.dtype)


def _post_block(h, mix, li, router_w, router_b, moe_w_gu, moe_b_gu, moe_w_down, moe_b_down, ln_g, ln_b):
    h = _layer_norm(DN_ALPHA * h + mix, ln_g[li, 0], ln_b[li, 0])
    f = _moe(h, router_w[li], router_b[li], moe_w_gu[li], moe_b_gu[li], moe_w_down[li], moe_b_down[li])
    return _layer_norm(DN_ALPHA * h + f, ln_g[li, 1], ln_b[li, 1])


def setup_inputs(seed: int = 0) -> dict:
    key = jax.random.key(seed)
    ks = iter(jax.random.split(key, 80))

    def nrm(shape, scale):
        return scale * jax.random.normal(next(ks), shape, jnp.float32)

    def unif(shape, lo, hi):
        return jax.random.uniform(next(ks), shape, jnp.float32, lo, hi)

    n_pages = PAST_LEN // PAGE_SIZE
    n_used = DEC_BATCH * n_pages
    n_pool = n_used + n_used // 4
    inp = {}
    inp['x_prompt'] = nrm((BATCH, SEQ, D_MODEL), 1.0)
    inp['x_sample'] = nrm((DEC_BATCH, DEC_SEQ, D_MODEL), 1.0)
    inp['state_a_re'] = nrm((L_EVEN, DEC_BATCH, A_GROUPS, A_STATE), 0.5)
    inp['state_a_im'] = nrm((L_EVEN, DEC_BATCH, A_GROUPS, A_STATE), 0.5)
    inp['cache_b_k'] = nrm((L_EVEN, n_pool, PAGE_SIZE, B_HEADS, 2 * B_DK), 1.0)
    inp['cache_b_v'] = nrm((L_EVEN, n_pool, PAGE_SIZE, B_HEADS, B_DV), 1.0)
    inp['state_c'] = nrm((L_ODD, DEC_BATCH, C_HEADS, C_DK, C_DV), 0.1)
    inp['state_c_conv'] = nrm((L_ODD, DEC_BATCH, C_CONV - 1, C_CONV_CH), 1.0)
    inp['cache_d_latent'] = nrm((L_ODD, n_pool, PAGE_SIZE, D_KV_LORA), 1.0)
    inp['cache_d_rope'] = nrm((L_ODD, n_pool, PAGE_SIZE, D_ROPE), 1.0)
    inp['page_table'] = jax.random.permutation(next(ks), n_pool)[:n_used].reshape(DEC_BATCH, n_pages).astype(jnp.int32)
    inp['w_in_even'] = nrm((L_EVEN, D_MODEL, EVEN_IN), D_MODEL ** -0.5)
    inp['s5_lam_re'] = -0.5 + nrm((L_EVEN, A_GROUPS, A_STATE), 0.01)
    inp['s5_lam_im'] = math.pi * jnp.arange(A_STATE, dtype=jnp.float32) + nrm((L_EVEN, A_GROUPS, A_STATE), 0.01)
    inp['s5_log_dt'] = unif((L_EVEN, A_GROUPS), math.log(1e-3), math.log(1e-1))
    inp['s5_b_re'] = nrm((L_EVEN, A_GROUPS, A_STATE, A_GROUP), (2 * A_GROUP) ** -0.5)
    inp['s5_b_im'] = nrm((L_EVEN, A_GROUPS, A_STATE, A_GROUP), (2 * A_GROUP) ** -0.5)
    inp['s5_c_re'] = nrm((L_EVEN, A_GROUPS, A_GROUP, A_STATE), (2 * A_STATE) ** -0.5)
    inp['s5_c_im'] = nrm((L_EVEN, A_GROUPS, A_GROUP, A_STATE), (2 * A_STATE) ** -0.5)
    inp['s5_d'] = nrm((L_EVEN, A_GROUPS, A_GROUP), 1.0)
    inp['s5_w_glu'] = nrm((L_EVEN, A_WIDTH, A_WIDTH), A_WIDTH ** -0.5)
    inp['diff_lq1'] = nrm((L_EVEN, B_DK), 0.1)
    inp['diff_lk1'] = nrm((L_EVEN, B_DK), 0.1)
    inp['diff_lq2'] = nrm((L_EVEN, B_DK), 0.1)
    inp['diff_lk2'] = nrm((L_EVEN, B_DK), 0.1)
    inp['diff_subln'] = 1.0 + nrm((L_EVEN, B_DV), 0.01)
    inp['w_out_even'] = nrm((L_EVEN, EVEN_OUT, D_MODEL), EVEN_OUT ** -0.5 * DN_BETA)
    inp['w_in_odd'] = nrm((L_ODD, D_MODEL, ODD_IN), D_MODEL ** -0.5)
    inp['gdn_conv_w'] = nrm((L_ODD, C_CONV, C_CONV_CH), C_CONV ** -0.5)
    inp['gdn_a_log'] = jnp.log(unif((L_ODD, C_HEADS), 1.0, 16.0))
    dt = jnp.exp(unif((L_ODD, C_HEADS), math.log(1e-3), math.log(1e-1)))
    inp['gdn_dt_bias'] = jnp.log(jnp.expm1(dt))
    inp['gdn_norm'] = 1.0 + nrm((L_ODD, C_DV), 0.01)
    inp['mla_q_norm'] = 1.0 + nrm((L_ODD, D_Q_LORA), 0.01)
    inp['mla_w_uq'] = nrm((L_ODD, D_Q_LORA, D_HEADS * (D_NOPE + D_ROPE)), D_Q_LORA ** -0.5)
    inp['mla_kv_norm'] = 1.0 + nrm((L_ODD, D_KV_LORA), 0.01)
    inp['mla_w_uk'] = nrm((L_ODD, D_HEADS, D_NOPE, D_KV_LORA), D_KV_LORA ** -0.5)
    inp['mla_w_uv'] = nrm((L_ODD, D_HEADS, D_KV_LORA, D_DV), D_KV_LORA ** -0.5)
    inp['w_out_odd'] = nrm((L_ODD, ODD_OUT, D_MODEL), ODD_OUT ** -0.5 * DN_BETA)
    inp['router_w'] = nrm((DEPTH, D_MODEL, N_EXPERTS), D_MODEL ** -0.5)
    inp['router_b'] = nrm((DEPTH, N_EXPERTS), 0.01)
    inp['moe_w_gu'] = nrm((DEPTH, N_EXPERTS, D_MODEL, 2 * D_FF), D_MODEL ** -0.5)
    inp['moe_b_gu'] = nrm((DEPTH, N_EXPERTS, 2 * D_FF), 0.01)
    inp['moe_w_down'] = nrm((DEPTH, N_EXPERTS, D_FF, D_MODEL), D_FF ** -0.5 * DN_BETA)
    inp['moe_b_down'] = nrm((DEPTH, N_EXPERTS, D_MODEL), 0.01)
    inp['ln_g'] = 1.0 + nrm((DEPTH, 2, D_MODEL), 0.01)
    inp['ln_b'] = nrm((DEPTH, 2, D_MODEL), 0.01)
    return inp


def reference(x_prompt, x_sample, state_a_re, state_a_im, cache_b_k, cache_b_v, state_c, state_c_conv,
              cache_d_latent, cache_d_rope, page_table, w_in_even, s5_lam_re, s5_lam_im, s5_log_dt,
              s5_b_re, s5_b_im, s5_c_re, s5_c_im, s5_d, s5_w_glu, diff_lq1, diff_lk1, diff_lq2, diff_lk2,
              diff_subln, w_out_even, w_in_odd, gdn_conv_w, gdn_a_log, gdn_dt_bias, gdn_norm, mla_q_norm,
              mla_w_uq, mla_kv_norm, mla_w_uk, mla_w_uv, w_out_odd, router_w, router_b, moe_w_gu, moe_b_gu,
              moe_w_down, moe_b_down, ln_g, ln_b):
    bp, lp, _ = x_prompt.shape
    bs, ls, _ = x_sample.shape
    pos_p = jnp.arange(lp, dtype=jnp.float32)
    pos_s = PAST_LEN + jnp.arange(ls, dtype=jnp.float32)
    hp, hs = x_prompt, x_sample
    pa_re, pa_im, pb_k, pb_v, pc_s, pc_conv, pd_lat, pd_rope = [], [], [], [], [], [], [], []
    sa_re, sa_im, sb_k, sb_v, sc_s, sc_conv, sd_lat, sd_rope = [], [], [], [], [], [], [], []
    for li in range(DEPTH):
        if li % 2 == 0:
            e = li // 2
            lam_init = 0.8 - 0.6 * math.exp(-0.3 * li)
            ew = (w_in_even[e], s5_lam_re[e], s5_lam_im[e], s5_log_dt[e], s5_b_re[e], s5_b_im[e],
                  s5_c_re[e], s5_c_im[e], s5_d[e], s5_w_glu[e], diff_lq1[e], diff_lk1[e], diff_lq2[e],
                  diff_lk2[e], diff_subln[e], w_out_even[e])
            z0 = jnp.zeros((bp, A_GROUPS, A_STATE), jnp.float32)
            mp, hr, hi, kk, vv = _even_mixer(hp, pos_p, z0, z0, _diff_attn_prompt, lam_init, *ew)
            pa_re.append(hr)
            pa_im.append(hi)
            pb_k.append(kk)
            pb_v.append(vv)
            attend_s = functools.partial(_diff_attn_sample, cache_k=cache_b_k, cache_v=cache_b_v,
                                         layer=e, page_table=page_table)
            ms, hr, hi, kk, vv = _even_mixer(hs, pos_s, state_a_re[e], state_a_im[e], attend_s, lam_init, *ew)
            sa_re.append(hr)
            sa_im.append(hi)
            sb_k.append(kk)
            sb_v.append(vv)
        else:
            o = li // 2
            ow = (w_in_odd[o], gdn_conv_w[o], gdn_a_log[o], gdn_dt_bias[o], gdn_norm[o], mla_q_norm[o],
                  mla_w_uq[o], mla_kv_norm[o], mla_w_uk[o], mla_w_uv[o], w_out_odd[o])
            conv0 = jnp.zeros((bp, C_CONV - 1, C_CONV_CH), hp.dtype)
            s0 = jnp.zeros((bp, C_HEADS, C_DK, C_DV), jnp.float32)
            mp, S, cb, cl, kr = _odd_mixer(hp, pos_p, conv0, s0, _mla_attn_prompt, *ow)
            pc_s.append(S)
            pc_conv.append(cb)
            pd_lat.append(cl)
            pd_rope.append(kr)
            attend_s = functools.partial(_mla_attn_sample, cache_lat=cache_d_latent, cache_rope=cache_d_rope,
                                         layer=o, page_table=page_table)
            ms, S, cb, cl, kr = _odd_mixer(hs, pos_s, state_c_conv[o], state_c[o], attend_s, *ow)
            sc_s.append(S)
            sc_conv.append(cb)
            sd_lat.append(cl)
            sd_rope.append(kr)
        hp = _post_block(hp, mp, li, router_w, router_b, moe_w_gu, moe_b_gu, moe_w_down, moe_b_down, ln_g, ln_b)
        hs = _post_block(hs, ms, li, router_w, router_b, moe_w_gu, moe_b_gu, moe_w_down, moe_b_down, ln_g, ln_b)
    p_a_re, p_a_im = jnp.stack(pa_re), jnp.stack(pa_im)
    p_b_k, p_b_v = jnp.stack(pb_k), jnp.stack(pb_v)
    p_c, p_c_conv = jnp.stack(pc_s), jnp.stack(pc_conv)
    p_d_lat, p_d_rope = jnp.stack(pd_lat), jnp.stack(pd_rope)
    s_a_re, s_a_im = jnp.stack(sa_re), jnp.stack(sa_im)
    s_b_k, s_b_v = jnp.stack(sb_k), jnp.stack(sb_v)
    s_c, s_c_conv = jnp.stack(sc_s), jnp.stack(sc_conv)
    s_d_lat, s_d_rope = jnp.stack(sd_lat), jnp.stack(sd_rope)
    return (hp, hs, p_a_re, p_a_im, p_b_k, p_b_v, p_c, p_c_conv, p_d_lat, p_d_rope,
            s_a_re, s_a_im, s_b_k, s_b_v, s_c, s_c_conv, s_d_lat, s_d_rope)
```

```python
import functools
import math

import jax
import jax.numpy as jnp
from jax import lax
from jax.experimental import pallas as pl
from jax.experimental.pallas import tpu as pltpu

D_MODEL = 1024
DEPTH = 2
PAST_LEN = 16384
PAGE_SIZE = 128
A_WIDTH = D_MODEL // 2
A_GROUP = 16
A_GROUPS = A_WIDTH // A_GROUP
A_STATE = 64
B_HEADS = 4
B_DK = 64
B_DV = 2 * B_DK
B_ROT = B_DK // 4
B_QK = B_HEADS * 2 * B_DK
C_HEADS = 4
C_DK = 128
C_DV = 128
C_CONV = 4
C_CHUNK = 64
C_QK = C_HEADS * C_DK
C_V = C_HEADS * C_DV
C_CONV_CH = 2 * C_QK + C_V
C_IN = C_CONV_CH + C_V + 2 * C_HEADS
D_HEADS = 4
D_NOPE = 128
D_ROPE = 32
D_DV = 128
D_Q_LORA = 384
D_KV_LORA = 256
ROPE_THETA = 500000.0
MLA_ROPE_THETA = 10000.0
N_EXPERTS = 32
TOP_K = 4
D_FF = 1024
SWIGLU_LIMIT = 7.0
SWIGLU_ALPHA = 1.702
DN_ALPHA = (2 * DEPTH) ** 0.25
EPS = 1e-6
NEG_BIG = -1e30

V7X_VMEM_BYTES = 64 * 1024 * 1024
VMEM_LIMIT = V7X_VMEM_BYTES * 7 // 8
BF16 = jnp.bfloat16
F32 = jnp.float32


def _cparams(sem):
    return pltpu.CompilerParams(dimension_semantics=sem, vmem_limit_bytes=VMEM_LIMIT)


def _mm_kernel(x_ref, w_ref, o_ref):
    o_ref[...] = jnp.dot(x_ref[...].astype(BF16), w_ref[...], preferred_element_type=F32)


def _mm(x, w_bf16, tm=256):
    m, k = x.shape
    n = w_bf16.shape[1]
    tm = min(tm, m)
    return pl.pallas_call(
        _mm_kernel,
        grid=(m // tm,),
        in_specs=[pl.BlockSpec((tm, k), lambda i: (i, 0)), pl.BlockSpec((k, n), lambda i: (0, 0))],
        out_specs=pl.BlockSpec((tm, n), lambda i: (i, 0)),
        out_shape=jax.ShapeDtypeStruct((m, n), F32),
        compiler_params=_cparams(("parallel",)),
        name="mm",
    )(x, w_bf16)


def _tri_pairs(n):
    qi = [i for i in range(n) for _ in range(i + 1)]
    kj = [j for i in range(n) for j in range(i + 1)]
    return jnp.asarray(qi, jnp.int32), jnp.asarray(kj, jnp.int32)


def _softmax_update(s, v_bf16, m_ref, l_ref, acc_ref, idx):
    m_prev = m_ref[idx]
    m_new = jnp.maximum(m_prev, jnp.max(s, axis=-1, keepdims=True))
    corr = jnp.exp(m_prev - m_new)
    p = jnp.exp(s - m_new)
    l_ref[idx] = corr * l_ref[idx] + jnp.sum(p, axis=-1, keepdims=True)
    acc_ref[idx] = corr * acc_ref[idx] + jnp.dot(p.astype(BF16), v_bf16, preferred_element_type=F32)
    m_ref[idx] = m_new


def _diff_attn_kernel(qi_ref, kj_ref, lam_ref, q_ref, k_ref, v_ref, g_ref, o_ref, m_ref, l_ref, acc_ref,
                      *, blk, out_scale):
    p = pl.program_id(2)
    qi = qi_ref[p]
    kj = kj_ref[p]

    @pl.when(kj == 0)
    def _():
        m_ref[...] = jnp.full(m_ref.shape, NEG_BIG, F32)
        l_ref[...] = jnp.zeros(l_ref.shape, F32)
        acc_ref[...] = jnp.zeros(acc_ref.shape, F32)

    def step(masked):
        q = (q_ref[...] * (B_DK ** -0.5)).astype(BF16)
        k = k_ref[...].astype(BF16)
        v = v_ref[...].astype(BF16)
        for c in range(2):
            s = lax.dot_general(q[:, c * B_DK:(c + 1) * B_DK], k[:, c * B_DK:(c + 1) * B_DK],
                                (((1,), (1,)), ((), ())), preferred_element_type=F32)
            if masked:
                row = lax.broadcasted_iota(jnp.int32, s.shape, 0)
                col = lax.broadcasted_iota(jnp.int32, s.shape, 1)
                s = jnp.where(col <= row, s, NEG_BIG)
            _softmax_update(s, v, m_ref, l_ref, acc_ref, c)

    @pl.when(kj < qi)
    def _():
        step(False)

    @pl.when(kj == qi)
    def _():
        step(True)
        lam = lam_ref[0, 0]
        o0 = acc_ref[0] / l_ref[0]
        o1 = acc_ref[1] / l_ref[1]
        y = o0 - lam * o1
        y = y * lax.rsqrt(jnp.mean(y * y, axis=-1, keepdims=True) + EPS)
        o_ref[...] = y * g_ref[...] * out_scale


def _diff_attn_prompt(q, k, v, lam, subln, bsz, seq, lam_init, blk=512):
    nq = seq // blk
    qi, kj = _tri_pairs(nq)
    kern = functools.partial(_diff_attn_kernel, blk=blk, out_scale=1.0 - lam_init)
    grid_spec = pltpu.PrefetchScalarGridSpec(
        num_scalar_prefetch=2,
        grid=(bsz, B_HEADS, qi.shape[0]),
        in_specs=[
            pl.BlockSpec(memory_space=pltpu.SMEM),
            pl.BlockSpec((blk, B_DV), lambda b, h, p, qi, kj: (b * nq + qi[p], h)),
            pl.BlockSpec((blk, B_DV), lambda b, h, p, qi, kj: (b * nq + kj[p], h)),
            pl.BlockSpec((blk, B_DV), lambda b, h, p, qi, kj: (b * nq + kj[p], h)),
            pl.BlockSpec((1, B_DV), lambda b, h, p, qi, kj: (0, 0)),
        ],
        out_specs=pl.BlockSpec((blk, B_DV), lambda b, h, p, qi, kj: (b * nq + qi[p], h)),
        scratch_shapes=[pltpu.VMEM((2, blk, 1), F32), pltpu.VMEM((2, blk, 1), F32),
                        pltpu.VMEM((2, blk, B_DV), F32)],
    )
    return pl.pallas_call(
        kern, grid_spec=grid_spec, out_shape=jax.ShapeDtypeStruct(q.shape, F32),
        compiler_params=_cparams(("parallel", "parallel", "arbitrary")), name="diff_attn_prompt",
    )(qi, kj, lam.reshape(1, 1), q, k, v, subln.reshape(1, B_DV))


def _mla_attn_kernel(qi_ref, kj_ref, ql_ref, qr_ref, c_ref, kr_ref, wuv_ref, o_ref, m_ref, l_ref, acc_ref, *, blk):
    p = pl.program_id(1)
    qi = qi_ref[p]
    kj = kj_ref[p]
    scale = (D_NOPE + D_ROPE) ** -0.5

    @pl.when(kj == 0)
    def _():
        m_ref[...] = jnp.full(m_ref.shape, NEG_BIG, F32)
        l_ref[...] = jnp.zeros(l_ref.shape, F32)
        acc_ref[...] = jnp.zeros(acc_ref.shape, F32)

    def step(masked):
        c = c_ref[...].astype(BF16)
        kr = kr_ref[...].astype(BF16)
        for h in range(D_HEADS):
            ql = (ql_ref[:, h * D_KV_LORA:(h + 1) * D_KV_LORA] * scale).astype(BF16)
            qr = (qr_ref[:, h * D_ROPE:(h + 1) * D_ROPE] * scale).astype(BF16)
            s = (lax.dot_general(ql, c, (((1,), (1,)), ((), ())), preferred_element_type=F32)
                 + lax.dot_general(qr, kr, (((1,), (1,)), ((), ())), preferred_element_type=F32))
            if masked:
                row = lax.broadcasted_iota(jnp.int32, s.shape, 0)
                col = lax.broadcasted_iota(jnp.int32, s.shape, 1)
                s = jnp.where(col <= row, s, NEG_BIG)
            _softmax_update(s, c, m_ref, l_ref, acc_ref, h)

    @pl.when(kj < qi)
    def _():
        step(False)

    @pl.when(kj == qi)
    def _():
        step(True)
        for h in range(D_HEADS):
            o = (acc_ref[h] / l_ref[h]).astype(BF16)
            o_ref[:, h * D_DV:(h + 1) * D_DV] = jnp.dot(o, wuv_ref[h], preferred_element_type=F32)


def _mla_attn_prompt(q_lat, q_rope, c, kr, w_uv_bf16, bsz, seq, blk=512):
    nq = seq // blk
    qi, kj = _tri_pairs(nq)
    grid_spec = pltpu.PrefetchScalarGridSpec(
        num_scalar_prefetch=2,
        grid=(bsz, qi.shape[0]),
        in_specs=[
            pl.BlockSpec((blk, D_HEADS * D_KV_LORA), lambda b, p, qi, kj: (b * nq + qi[p], 0)),
            pl.BlockSpec((blk, D_HEADS * D_ROPE), lambda b, p, qi, kj: (b * nq + qi[p], 0)),
            pl.BlockSpec((blk, D_KV_LORA), lambda b, p, qi, kj: (b * nq + kj[p], 0)),
            pl.BlockSpec((blk, D_ROPE), lambda b, p, qi, kj: (b * nq + kj[p], 0)),
            pl.BlockSpec((D_HEADS, D_KV_LORA, D_DV), lambda b, p, qi, kj: (0, 0, 0)),
        ],
        out_specs=pl.BlockSpec((blk, D_HEADS * D_DV), lambda b, p, qi, kj: (b * nq + qi[p], 0)),
        scratch_shapes=[pltpu.VMEM((D_HEADS, blk, 1), F32), pltpu.VMEM((D_HEADS, blk, 1), F32),
                        pltpu.VMEM((D_HEADS, blk, D_KV_LORA), F32)],
    )
    return pl.pallas_call(
        functools.partial(_mla_attn_kernel, blk=blk), grid_spec=grid_spec,
        out_shape=jax.ShapeDtypeStruct((q_lat.shape[0], D_HEADS * D_DV), F32),
        compiler_params=_cparams(("parallel", "arbitrary")), name="mla_attn_prompt",
    )(qi, kj, q_lat, q_rope, c, kr, w_uv_bf16)


def _moe_kernel(be_ref, nu_ref, x_ref, wgu_ref, bgu_ref, wd_ref, bd_ref, o_ref, wgu_bf, wd_bf):
    i = pl.program_id(0)
    e = be_ref[i]
    e_prev = be_ref[jnp.maximum(i - 1, 0)]

    @pl.when((i == 0) | (e != e_prev))
    def _():
        wgu_bf[...] = wgu_ref[0].astype(BF16)
        wd_bf[...] = wd_ref[0].astype(BF16)

    @pl.when(i < nu_ref[0])
    def _():
        h = jnp.dot(x_ref[...].astype(BF16), wgu_bf[...], preferred_element_type=F32) + bgu_ref[0]
        gate = jnp.minimum(h[:, :D_FF], SWIGLU_LIMIT)
        up = jnp.clip(h[:, D_FF:], -SWIGLU_LIMIT, SWIGLU_LIMIT)
        act = (up + 1.0) * gate * jax.nn.sigmoid(SWIGLU_ALPHA * gate)
        o_ref[...] = jnp.dot(act.astype(BF16), wd_bf[...], preferred_element_type=F32) + bd_ref[0]

    @pl.when(i >= nu_ref[0])
    def _():
        o_ref[...] = jnp.zeros(o_ref.shape, F32)


def _moe_experts(x_sorted, block_e, n_used, w_gu, b_gu, w_down, b_down, blk):
    p_rows, d = x_sorted.shape
    n_blocks = p_rows // blk
    grid_spec = pltpu.PrefetchScalarGridSpec(
        num_scalar_prefetch=2,
        grid=(n_blocks,),
        in_specs=[
            pl.BlockSpec((blk, d), lambda i, be, nu: (i, 0)),
            pl.BlockSpec((1, d, 2 * D_FF), lambda i, be, nu: (be[i], 0, 0)),
            pl.BlockSpec((1, 1, 2 * D_FF), lambda i, be, nu: (be[i], 0, 0)),
            pl.BlockSpec((1, D_FF, d), lambda i, be, nu: (be[i], 0, 0)),
            pl.BlockSpec((1, 1, d), lambda i, be, nu: (be[i], 0, 0)),
        ],
        out_specs=pl.BlockSpec((blk, d), lambda i, be, nu: (i, 0)),
        scratch_shapes=[pltpu.VMEM((d, 2 * D_FF), BF16), pltpu.VMEM((D_FF, d), BF16)],
    )
    return pl.pallas_call(
        _moe_kernel, grid_spec=grid_spec, out_shape=jax.ShapeDtypeStruct((p_rows, d), F32),
        compiler_params=_cparams(("arbitrary",)), name="moe_experts",
    )(block_e, n_used, x_sorted, w_gu, b_gu.reshape(N_EXPERTS, 1, 2 * D_FF), w_down,
      b_down.reshape(N_EXPERTS, 1, d))


def _moe(xt, router_w, router_b, w_gu, b_gu, w_down, b_down):
    n, d = xt.shape
    blk = 256 if n >= 8192 else 128
    logits = jnp.dot(xt, router_w, precision=lax.Precision.HIGHEST) + router_b
    top_val, top_idx = lax.top_k(logits, TOP_K)
    gates = jax.nn.softmax(top_val, axis=-1)
    nk = n * TOP_K
    flat_e = top_idx.reshape(-1).astype(jnp.int32)
    onehot = (flat_e[:, None] == jnp.arange(N_EXPERTS, dtype=jnp.int32)[None, :]).astype(jnp.int32)
    csum = jnp.cumsum(onehot, axis=0)
    counts = csum[-1]
    rank = jnp.sum((csum - onehot) * onehot, axis=1)
    padded = (counts + blk - 1) // blk * blk
    pend = jnp.cumsum(padded)
    pstart = pend - padded
    dest = pstart[flat_e] + rank
    n_blocks = -(-(nk + N_EXPERTS * (blk - 1)) // blk)
    p_rows = n_blocks * blk
    buf_tok = jnp.full((p_rows,), n, jnp.int32).at[dest].set(jnp.arange(nk, dtype=jnp.int32) // TOP_K,
                                                               unique_indices=True)
    block_e = jnp.minimum(jnp.searchsorted(pend, jnp.arange(n_blocks, dtype=jnp.int32) * blk, side='right'),
                          N_EXPERTS - 1).astype(jnp.int32)
    n_used = (pend[-1] // blk).astype(jnp.int32).reshape(1)
    x_pad = jnp.concatenate([xt, jnp.zeros((1, d), xt.dtype)], axis=0)
    x_sorted = x_pad[buf_tok]
    y_buf = _moe_experts(x_sorted, block_e, n_used, w_gu, b_gu, w_down, b_down, blk)
    y_assign = y_buf[dest].reshape(n, TOP_K, d)
    return jnp.einsum('nkd,nk->nd', y_assign, gates)


def _rms_norm(x, g):
    return x * lax.rsqrt(jnp.mean(x * x, axis=-1, keepdims=True) + EPS) * g


def _layer_norm(x, g, b):
    mu = jnp.mean(x, axis=-1, keepdims=True)
    var = jnp.mean(jnp.square(x - mu), axis=-1, keepdims=True)
    return (x - mu) * lax.rsqrt(var + EPS) * g + b


def _rope(x, pos, theta, n_rot):
    half = n_rot // 2
    inv = jnp.exp(-math.log(theta) * jnp.arange(half, dtype=F32) * (2.0 / n_rot))
    ang = pos[:, None] * inv[None, :]
    cos = jnp.cos(ang)[:, None, :]
    sin = jnp.sin(ang)[:, None, :]
    x1 = x[..., :half]
    x2 = x[..., half:n_rot]
    return jnp.concatenate([x1 * cos - x2 * sin, x2 * cos + x1 * sin, x[..., n_rot:]], axis=-1)


def _online_softmax_step(carry, s, vals, pv):
    m, l, acc = carry
    m_new = jnp.maximum(m, jnp.max(s, axis=-1))
    corr = jnp.exp(m - m_new)
    p = jnp.exp(s - m_new[..., None])
    return (m_new, l * corr + jnp.sum(p, axis=-1), acc * corr[..., None] + pv(p, vals))


def _s5_scan(u, h0_re, h0_im, lam_re, lam_im, log_dt, b_re, b_im, c_re, c_im, d_skip):
    bsz, L, _ = u.shape
    uf = u.reshape(bsz, L, A_GROUPS, A_GROUP)
    dt = jnp.exp(log_dt)[:, None]
    mag = jnp.exp(lam_re * dt)
    ar = mag * jnp.cos(lam_im * dt)
    ai = mag * jnp.sin(lam_im * dt)
    den = lam_re * lam_re + lam_im * lam_im
    fr = ((ar - 1.0) * lam_re + ai * lam_im) / den
    fi = (ai * lam_re - (ar - 1.0) * lam_im) / den
    bbr = fr[..., None] * b_re - fi[..., None] * b_im
    bbi = fr[..., None] * b_im + fi[..., None] * b_re
    xr = jnp.einsum('blgp,gnp->blgn', uf, bbr)
    xi = jnp.einsum('blgp,gnp->blgn', uf, bbi)
    xr = xr.at[:, 0].add(ar * h0_re - ai * h0_im)
    xi = xi.at[:, 0].add(ar * h0_im + ai * h0_re)
    a_r = jnp.broadcast_to(ar, xr.shape)
    a_i = jnp.broadcast_to(ai, xi.shape)

    def combine(e1, e2):
        a1r, a1i, b1r, b1i = e1
        a2r, a2i, b2r, b2i = e2
        return (a2r * a1r - a2i * a1i, a2r * a1i + a2i * a1r,
                a2r * b1r - a2i * b1i + b2r, a2r * b1i + a2i * b1r + b2i)

    _, _, hr, hi = lax.associative_scan(combine, (a_r, a_i, xr, xi), axis=1)
    y = (jnp.einsum('gpn,blgn->blgp', c_re, hr) - jnp.einsum('gpn,blgn->blgp', c_im, hi) + d_skip * uf)
    return y.reshape(bsz, L, A_WIDTH), hr[:, -1], hi[:, -1]


def _diff_attn_sample(q, k, v, cache_k, cache_v, layer, page_table):
    bd, T = q.shape[:2]
    scale = B_DK ** -0.5

    def pv(p, vals):
        return jnp.einsum('bhcts,bshd->bhctd', p, vals)

    def page_step(carry, pages):
        kp = cache_k[layer, pages].reshape(bd, PAGE_SIZE, B_HEADS, 2, B_DK)
        vp = cache_v[layer, pages]
        s = jnp.einsum('bthcd,bshcd->bhcts', q, kp, preferred_element_type=F32) * scale
        return _online_softmax_step(carry, s, vp, pv), None

    init = (jnp.full((bd, B_HEADS, 2, T), NEG_BIG, F32), jnp.zeros((bd, B_HEADS, 2, T), F32),
            jnp.zeros((bd, B_HEADS, 2, T, B_DV), F32))
    carry, _ = lax.scan(page_step, init, page_table.T)
    tq = jnp.arange(T)
    s_self = jnp.einsum('bthcd,bshcd->bhcts', q, k, preferred_element_type=F32) * scale
    s_self = jnp.where(tq[None, :] <= tq[:, None], s_self, -jnp.inf)
    _, l, acc = _online_softmax_step(carry, s_self, v, pv)
    o = acc / l[..., None]
    return o.transpose(0, 3, 1, 2, 4)


def _even_mixer(x, pos, h0_re, h0_im, sample_ctx, lam_init, w_in, lam_re, lam_im, log_dt, b_re, b_im,
                c_re, c_im, d_skip, w_glu, lq1, lk1, lq2, lk2, subln, w_out):
    bsz, L, _ = x.shape
    n = bsz * L
    proj = _mm(x.reshape(n, D_MODEL), w_in.astype(BF16))
    u = proj[:, :A_WIDTH].reshape(bsz, L, A_WIDTH)
    q = proj[:, A_WIDTH:A_WIDTH + B_QK]
    k = proj[:, A_WIDTH + B_QK:A_WIDTH + 2 * B_QK]
    v = proj[:, A_WIDTH + 2 * B_QK:]
    y_a, h_re, h_im = _s5_scan(u, h0_re, h0_im, lam_re, lam_im, log_dt, b_re, b_im, c_re, c_im, d_skip)
    y_a = jax.nn.gelu(y_a).reshape(n, A_WIDTH)
    y_a = y_a * jax.nn.sigmoid(_mm(y_a, w_glu.astype(BF16)))
    q = _rope(q.reshape(bsz, L, 2 * B_HEADS, B_DK), pos, ROPE_THETA, B_ROT)
    k = _rope(k.reshape(bsz, L, 2 * B_HEADS, B_DK), pos, ROPE_THETA, B_ROT)
    lam = jnp.exp(jnp.sum(lq1 * lk1)) - jnp.exp(jnp.sum(lq2 * lk2)) + lam_init
    if sample_ctx is None:
        y_b = _diff_attn_prompt(q.reshape(n, B_QK), k.reshape(n, B_QK), v, lam, subln, bsz, L, lam_init)
    else:
        cache_k, cache_v, layer, page_table = sample_ctx
        o = _diff_attn_sample(q.reshape(bsz, L, B_HEADS, 2, B_DK), k.reshape(bsz, L, B_HEADS, 2, B_DK),
                              v.reshape(bsz, L, B_HEADS, B_DV), cache_k, cache_v, layer, page_table)
        y_b = o[..., 0, :] - lam * o[..., 1, :]
        y_b = (_rms_norm(y_b, subln) * (1.0 - lam_init)).reshape(n, B_HEADS * B_DV)
    y = _mm(jnp.concatenate([y_a, y_b], axis=-1), w_out.astype(BF16))
    return (y.reshape(bsz, L, D_MODEL), h_re, h_im, k.reshape(bsz, L, B_HEADS, 2 * B_DK),
            v.reshape(bsz, L, B_HEADS, B_DV))


def _gated_delta_chunked(q, k, v, beta, g, s0):
    bsz, L, H, _ = q.shape
    C = min(C_CHUNK, L)
    nc = -(-L // C)
    pad = nc * C - L

    def blocks(t):
        t = jnp.pad(t, [(0, 0), (0, pad)] + [(0, 0)] * (t.ndim - 2))
        t = t.reshape((bsz, nc, C) + t.shape[2:])
        return jnp.moveaxis(t, 3, 2).swapaxes(0, 1)

    qc, kc, vc, bc, gc = blocks(q), blocks(k), blocks(v), blocks(beta), blocks(g)
    g_cum = jnp.cumsum(gc, axis=-1)
    idx = jnp.arange(C)
    causal = idx[:, None] >= idx[None, :]
    strict = idx[:, None] > idx[None, :]
    decay = jnp.exp(jnp.where(causal, g_cum[..., :, None] - g_cum[..., None, :], -jnp.inf))
    k_beta = kc * bc[..., None]
    v_beta = vc * bc[..., None]
    m = jnp.where(strict, jnp.einsum('nbhik,nbhjk->nbhij', k_beta, kc) * decay, 0.0)
    rhs = jnp.concatenate([v_beta, k_beta * jnp.exp(g_cum)[..., None]], axis=-1)
    sol = lax.linalg.triangular_solve(m, rhs, left_side=True, lower=True, unit_diagonal=True)
    u_c = sol[..., :C_DV]
    w_c = sol[..., C_DV:]
    a_intra = jnp.einsum('nbhik,nbhjk->nbhij', qc, kc) * decay

    def step(S, xs):
        q_i, k_i, u_i, w_i, g_i, a_i = xs
        v_new = u_i - jnp.einsum('bhck,bhkv->bhcv', w_i, S)
        o_i = (jnp.einsum('bhck,bhkv->bhcv', q_i * jnp.exp(g_i)[..., None], S)
               + jnp.einsum('bhij,bhjv->bhiv', a_i, v_new))
        g_last = g_i[..., -1:]
        S = (S * jnp.exp(g_last)[..., None]
             + jnp.einsum('bhck,bhcv->bhkv', k_i * jnp.exp(g_last - g_i)[..., None], v_new))
        return S, o_i

    S, o = lax.scan(step, s0, (qc, kc, u_c, w_c, g_cum, a_intra))
    o = jnp.moveaxis(o.swapaxes(0, 1), 2, 3).reshape(bsz, nc * C, H, C_DV)[:, :L]
    return o, S


def _gdn_mixer(p, conv0, s0, conv_w, a_log, dt_bias, norm_g):
    bsz, L, _ = p.shape
    qkv = p[..., :C_CONV_CH]
    z = p[..., C_CONV_CH:C_CONV_CH + C_V]
    b = p[..., C_CONV_CH + C_V:C_CONV_CH + C_V + C_HEADS]
    a = p[..., C_CONV_CH + C_V + C_HEADS:]
    xp = jnp.concatenate([conv0, qkv], axis=1)
    conv = sum(conv_w[j] * xp[:, j:j + L] for j in range(C_CONV))
    new_buf = xp[:, xp.shape[1] - (C_CONV - 1):]
    act = jax.nn.silu(conv)
    q = act[..., :C_QK].reshape(bsz, L, C_HEADS, C_DK)
    k = act[..., C_QK:2 * C_QK].reshape(bsz, L, C_HEADS, C_DK)
    v = act[..., 2 * C_QK:].reshape(bsz, L, C_HEADS, C_DV)
    q = q * lax.rsqrt(jnp.sum(q * q, axis=-1, keepdims=True) + EPS) * (C_DK ** -0.5)
    k = k * lax.rsqrt(jnp.sum(k * k, axis=-1, keepdims=True) + EPS)
    beta = jax.nn.sigmoid(b)
    g = -jnp.exp(a_log) * jax.nn.softplus(a + dt_bias)
    o, S = _gated_delta_chunked(q, k, v, beta, g, s0)
    o = _rms_norm(o, norm_g) * jax.nn.silu(z.reshape(bsz, L, C_HEADS, C_DV))
    return o.reshape(bsz, L, C_V), S, new_buf


def _mla_attn_sample(q_lat, q_rope, c, kr, cache_lat, cache_rope, layer, page_table):
    bd, T = q_lat.shape[:2]
    scale = (D_NOPE + D_ROPE) ** -0.5

    def pv(p, vals):
        return jnp.einsum('bhts,bsr->bhtr', p, vals)

    def scores(cc, rr):
        return (jnp.einsum('bthr,bsr->bhts', q_lat, cc, preferred_element_type=F32)
                + jnp.einsum('bthd,bsd->bhts', q_rope, rr, preferred_element_type=F32)) * scale

    def page_step(carry, pages):
        cc = cache_lat[layer, pages]
        rr = cache_rope[layer, pages]
        return _online_softmax_step(carry, scores(cc, rr), cc, pv), None

    init = (jnp.full((bd, D_HEADS, T), NEG_BIG, F32), jnp.zeros((bd, D_HEADS, T), F32),
            jnp.zeros((bd, D_HEADS, T, D_KV_LORA), F32))
    carry, _ = lax.scan(page_step, init, page_table.T)
    tq = jnp.arange(T)
    s_self = jnp.where(tq[None, :] <= tq[:, None], scores(c, kr), -jnp.inf)
    _, l, acc = _online_softmax_step(carry, s_self, c, pv)
    return (acc / l[..., None]).transpose(0, 2, 1, 3)


def _odd_mixer(x, pos, conv0, s0, sample_ctx, w_in, conv_w, a_log, dt_bias, gdn_g,
               q_norm, w_uq, kv_norm, w_uk, w_uv, w_out):
    bsz, L, _ = x.shape
    n = bsz * L
    odd_in = w_in.shape[1]
    pad = -odd_in % 128
    w_in_p = jnp.pad(w_in, ((0, 0), (0, pad))).astype(BF16)
    proj = _mm(x.reshape(n, D_MODEL), w_in_p)[:, :odd_in].reshape(bsz, L, odd_in)
    y_c, S, conv_new = _gdn_mixer(proj[..., :C_IN], conv0, s0, conv_w, a_log, dt_bias, gdn_g)
    pm = proj[..., C_IN:]
    c_q = pm[..., :D_Q_LORA]
    c_kv = pm[..., D_Q_LORA:D_Q_LORA + D_KV_LORA]
    k_r = pm[..., D_Q_LORA + D_KV_LORA:]
    qf = _mm(_rms_norm(c_q, q_norm).reshape(n, D_Q_LORA), w_uq.astype(BF16))
    qf = qf.reshape(bsz, L, D_HEADS, D_NOPE + D_ROPE)
    q_rope = _rope(qf[..., D_NOPE:], pos, MLA_ROPE_THETA, D_ROPE)
    q_nope = qf[..., :D_NOPE].reshape(n, D_HEADS * D_NOPE)
    w_bd = jnp.zeros((D_HEADS * D_NOPE, D_HEADS * D_KV_LORA), F32)
    for h in range(D_HEADS):
        w_bd = w_bd.at[h * D_NOPE:(h + 1) * D_NOPE, h * D_KV_LORA:(h + 1) * D_KV_LORA].set(w_uk[h])
    q_lat = _mm(q_nope, w_bd.astype(BF16))
    c = _rms_norm(c_kv, kv_norm)
    kr = _rope(k_r[:, :, None, :], pos, MLA_ROPE_THETA, D_ROPE)[:, :, 0, :]
    if sample_ctx is None:
        y_d = _mla_attn_prompt(q_lat, q_rope.reshape(n, D_HEADS * D_ROPE), c.reshape(n, D_KV_LORA),
                               kr.reshape(n, D_ROPE), w_uv.astype(BF16), bsz, L)
    else:
        cache_lat, cache_rope, layer, page_table = sample_ctx
        o_lat = _mla_attn_sample(q_lat.reshape(bsz, L, D_HEADS, D_KV_LORA), q_rope, c, kr, cache_lat, cache_rope,
                                 layer, page_table)
        w_uv_bd = jnp.zeros((D_HEADS * D_KV_LORA, D_HEADS * D_DV), F32)
        for h in range(D_HEADS):
            w_uv_bd = w_uv_bd.at[h * D_KV_LORA:(h + 1) * D_KV_LORA, h * D_DV:(h + 1) * D_DV].set(w_uv[h])
        y_d = _mm(o_lat.reshape(n, D_HEADS * D_KV_LORA), w_uv_bd.astype(BF16))
    y = _mm(jnp.concatenate([y_c.reshape(n, C_V), y_d], axis=-1), w_out.astype(BF16))
    return y.reshape(bsz, L, D_MODEL), S, conv_new, c, kr


def _post_block(h, mix, li, router_w, router_b, moe_w_gu, moe_b_gu, moe_w_down, moe_b_down, ln_g, ln_b):
    bsz, L, d = h.shape
    h = _layer_norm(DN_ALPHA * h + mix, ln_g[li, 0], ln_b[li, 0])
    f = _moe(h.reshape(-1, d), router_w[li], router_b[li], moe_w_gu[li], moe_b_gu[li], moe_w_down[li],
             moe_b_down[li]).reshape(bsz, L, d)
    return _layer_norm(DN_ALPHA * h + f, ln_g[li, 1], ln_b[li, 1])


def kernel(x_prompt, x_sample, state_a_re, state_a_im, cache_b_k, cache_b_v, state_c, state_c_conv,
           cache_d_latent, cache_d_rope, page_table, w_in_even, s5_lam_re, s5_lam_im, s5_log_dt,
           s5_b_re, s5_b_im, s5_c_re, s5_c_im, s5_d, s5_w_glu, diff_lq1, diff_lk1, diff_lq2, diff_lk2,
           diff_subln, w_out_even, w_in_odd, gdn_conv_w, gdn_a_log, gdn_dt_bias, gdn_norm, mla_q_norm,
           mla_w_uq, mla_kv_norm, mla_w_uk, mla_w_uv, w_out_odd, router_w, router_b, moe_w_gu, moe_b_gu,
           moe_w_down, moe_b_down, ln_g, ln_b):
    bp, lp, _ = x_prompt.shape
    bs, ls, _ = x_sample.shape
    pos_p = jnp.arange(lp, dtype=F32)
    pos_s = PAST_LEN + jnp.arange(ls, dtype=F32)
    hp, hs = x_prompt, x_sample
    outs_p = {k: [] for k in ('a_re', 'a_im', 'b_k', 'b_v', 'c', 'c_conv', 'd_lat', 'd_rope')}
    outs_s = {k: [] for k in outs_p}
    for li in range(DEPTH):
        if li % 2 == 0:
            e = li // 2
            lam_init = 0.8 - 0.6 * math.exp(-0.3 * li)
            ew = (w_in_even[e], s5_lam_re[e], s5_lam_im[e], s5_log_dt[e], s5_b_re[e], s5_b_im[e],
                  s5_c_re[e], s5_c_im[e], s5_d[e], s5_w_glu[e], diff_lq1[e], diff_lk1[e], diff_lq2[e],
                  diff_lk2[e], diff_subln[e], w_out_even[e])
            z0 = jnp.zeros((bp, A_GROUPS, A_STATE), F32)
            mp, hr, hi, kk, vv = _even_mixer(hp, pos_p, z0, z0, None, lam_init, *ew)
            for key, val in zip(('a_re', 'a_im', 'b_k', 'b_v'), (hr, hi, kk, vv)):
                outs_p[key].append(val)
            ctx = (cache_b_k, cache_b_v, e, page_table)
            ms, hr, hi, kk, vv = _even_mixer(hs, pos_s, state_a_re[e], state_a_im[e], ctx, lam_init, *ew)
            for key, val in zip(('a_re', 'a_im', 'b_k', 'b_v'), (hr, hi, kk, vv)):
                outs_s[key].append(val)
        else:
            o = li // 2
            ow = (w_in_odd[o], gdn_conv_w[o], gdn_a_log[o], gdn_dt_bias[o], gdn_norm[o], mla_q_norm[o],
                  mla_w_uq[o], mla_kv_norm[o], mla_w_uk[o], mla_w_uv[o], w_out_odd[o])
            conv0 = jnp.zeros((bp, C_CONV - 1, C_CONV_CH), F32)
            s0 = jnp.zeros((bp, C_HEADS, C_DK, C_DV), F32)
            mp, S, cb, cl, kr = _odd_mixer(hp, pos_p, conv0, s0, None, *ow)
            for key, val in zip(('c', 'c_conv', 'd_lat', 'd_rope'), (S, cb, cl, kr)):
                outs_p[key].append(val)
            ctx = (cache_d_latent, cache_d_rope, o, page_table)
            ms, S, cb, cl, kr = _odd_mixer(hs, pos_s, state_c_conv[o], state_c[o], ctx, *ow)
            for key, val in zip(('c', 'c_conv', 'd_lat', 'd_rope'), (S, cb, cl, kr)):
                outs_s[key].append(val)
        post = (li, router_w, router_b, moe_w_gu, moe_b_gu, moe_w_down, moe_b_down, ln_g, ln_b)
        hp = _post_block(hp, mp, *post)
        hs = _post_block(hs, ms, *post)
    keys = ('a_re', 'a_im', 'b_k', 'b_v', 'c', 'c_conv', 'd_lat', 'd_rope')
    return ((hp, hs) + tuple(jnp.stack(outs_p[k]) for k in keys) + tuple(jnp.stack(outs_s[k]) for k in keys))
```

```python
import functools
import math

import jax
import jax.numpy as jnp
from jax import lax
from jax.experimental import pallas as pl
from jax.experimental.pallas import tpu as pltpu

D_MODEL = 1024
DEPTH = 2
PAST_LEN = 16384
PAGE_SIZE = 128
A_WIDTH = D_MODEL // 2
A_GROUP = 16
A_GROUPS = A_WIDTH // A_GROUP
A_STATE = 64
B_HEADS = 4
B_DK = 64
B_DV = 2 * B_DK
B_ROT = B_DK // 4
B_QK = B_HEADS * 2 * B_DK
C_HEADS = 4
C_DK = 128
C_DV = 128
C_CONV = 4
C_CHUNK = 64
C_QK = C_HEADS * C_DK
C_V = C_HEADS * C_DV
C_CONV_CH = 2 * C_QK + C_V
C_IN = C_CONV_CH + C_V + 2 * C_HEADS
D_HEADS = 4
D_NOPE = 128
D_ROPE = 32
D_DV = 128
D_Q_LORA = 384
D_KV_LORA = 256
ROPE_THETA = 500000.0
MLA_ROPE_THETA = 10000.0
N_EXPERTS = 32
TOP_K = 4
D_FF = 1024
SWIGLU_LIMIT = 7.0
SWIGLU_ALPHA = 1.702
DN_ALPHA = (2 * DEPTH) ** 0.25
EPS = 1e-6
NEG_BIG = -1e30

V7X_VMEM_BYTES = 64 * 1024 * 1024
VMEM_LIMIT = V7X_VMEM_BYTES * 7 // 8
BF16 = jnp.bfloat16
F32 = jnp.float32


def _cparams(sem):
    return pltpu.CompilerParams(dimension_semantics=sem, vmem_limit_bytes=VMEM_LIMIT)


def _mm_kernel(x_ref, w_ref, o_ref):
    o_ref[...] = jnp.dot(x_ref[...].astype(BF16), w_ref[...], preferred_element_type=F32)


def _mm(x, w_bf16, tm=256):
    m, k = x.shape
    n = w_bf16.shape[1]
    tm = min(tm, m)
    return pl.pallas_call(
        _mm_kernel,
        grid=(m // tm,),
        in_specs=[pl.BlockSpec((tm, k), lambda i: (i, 0)), pl.BlockSpec((k, n), lambda i: (0, 0))],
        out_specs=pl.BlockSpec((tm, n), lambda i: (i, 0)),
        out_shape=jax.ShapeDtypeStruct((m, n), F32),
        compiler_params=_cparams(("parallel",)),
        name="mm",
    )(x, w_bf16)


S5_HALF_IN = A_WIDTH // 2
S5_HALF_ST = A_GROUPS * A_STATE // 2
S5_STATES = A_GROUPS * A_STATE


def _s5_kernel(u_ref, h0r_ref, h0i_ref, a_ref, win_ref, cre_ref, cim_ref, d_ref, wglu_ref,
               y_ref, hr_out, hi_out, x_sc, hr_sc, hi_sc, *, nb, steps):
    j = pl.program_id(0)

    @pl.when(j == 0)
    def _():
        hr_sc[...] = h0r_ref[...]
        hi_sc[...] = h0i_ref[...]

    rows = steps * nb
    u = u_ref[...].reshape(rows, A_WIDTH)
    ub = u.astype(BF16)
    for k in range(2):
        xk = jnp.dot(ub[:, k * S5_HALF_IN:(k + 1) * S5_HALF_IN], win_ref[k], preferred_element_type=F32)
        x_sc[:, k * S5_HALF_ST:(k + 1) * S5_HALF_ST] = xk[:, :S5_HALF_ST]
        x_sc[:, S5_STATES + k * S5_HALF_ST:S5_STATES + (k + 1) * S5_HALF_ST] = xk[:, S5_HALF_ST:]
    ar = a_ref[0:1, :]
    ai = a_ref[1:2, :]

    def body(t, carry):
        r = pl.ds(pl.multiple_of(t * nb, nb), nb)
        hr = hr_sc[...]
        hi = hi_sc[...]
        nr = ar * hr - ai * hi + x_sc[r, 0:S5_STATES]
        ni = ar * hi + ai * hr + x_sc[r, S5_STATES:2 * S5_STATES]
        x_sc[r, 0:S5_STATES] = nr
        x_sc[r, S5_STATES:2 * S5_STATES] = ni
        hr_sc[...] = nr
        hi_sc[...] = ni
        return carry

    lax.fori_loop(0, steps, body, 0)
    ys = []
    for k in range(2):
        hk_r = x_sc[:, k * S5_HALF_ST:(k + 1) * S5_HALF_ST].astype(BF16)
        hk_i = x_sc[:, S5_STATES + k * S5_HALF_ST:S5_STATES + (k + 1) * S5_HALF_ST].astype(BF16)
        yk = (jnp.dot(hk_r, cre_ref[k], preferred_element_type=F32)
              - jnp.dot(hk_i, cim_ref[k], preferred_element_type=F32))
        ys.append(yk)
    y = jnp.concatenate(ys, axis=-1) + d_ref[...] * u
    y = jax.nn.gelu(y)
    y = y * jax.nn.sigmoid(jnp.dot(y.astype(BF16), wglu_ref[...], preferred_element_type=F32))
    y_ref[...] = y.reshape(steps, nb, A_WIDTH)

    @pl.when(j == pl.num_programs(0) - 1)
    def _():
        hr_out[...] = hr_sc[...]
        hi_out[...] = hi_sc[...]


def _s5_weights(lam_re, lam_im, log_dt, b_re, b_im, c_re, c_im):
    dt = jnp.exp(log_dt)[:, None]
    mag = jnp.exp(lam_re * dt)
    ar = mag * jnp.cos(lam_im * dt)
    ai = mag * jnp.sin(lam_im * dt)
    den = lam_re * lam_re + lam_im * lam_im
    fr = ((ar - 1.0) * lam_re + ai * lam_im) / den
    fi = (ai * lam_re - (ar - 1.0) * lam_im) / den
    bbr = fr[..., None] * b_re - fi[..., None] * b_im
    bbi = fr[..., None] * b_im + fi[..., None] * b_re
    eye = jnp.eye(A_GROUPS, dtype=F32)

    def in_map(bb):
        return jnp.einsum('gh,gnp->gphn', eye, bb).reshape(A_WIDTH, S5_STATES)

    def out_map(c):
        return jnp.einsum('gh,gpn->gnhp', eye, c).reshape(S5_STATES, A_WIDTH)

    wr, wi = in_map(bbr), in_map(bbi)
    win = jnp.stack([jnp.concatenate([w[k * S5_HALF_IN:(k + 1) * S5_HALF_IN, k * S5_HALF_ST:(k + 1) * S5_HALF_ST]
                                      for w in (wr, wi)], axis=1) for k in range(2)]).astype(BF16)
    cr, ci = out_map(c_re), out_map(c_im)
    cre = jnp.stack([cr[k * S5_HALF_ST:(k + 1) * S5_HALF_ST, k * S5_HALF_IN:(k + 1) * S5_HALF_IN]
                     for k in range(2)]).astype(BF16)
    cim = jnp.stack([ci[k * S5_HALF_ST:(k + 1) * S5_HALF_ST, k * S5_HALF_IN:(k + 1) * S5_HALF_IN]
                     for k in range(2)]).astype(BF16)
    a = jnp.stack([ar.reshape(-1), ai.reshape(-1)])
    return a, win, cre, cim


def _s5_glu(u_tb, h0_re, h0_im, weights, d_skip, w_glu_bf16, steps):
    seq, nb, _ = u_tb.shape
    a, win, cre, cim = weights
    rows = steps * nb
    const = lambda *shape: pl.BlockSpec(shape, lambda j: (0,) * len(shape))
    return pl.pallas_call(
        functools.partial(_s5_kernel, nb=nb, steps=steps),
        grid=(seq // steps,),
        in_specs=[pl.BlockSpec((steps, nb, A_WIDTH), lambda j: (j, 0, 0)),
                  const(nb, S5_STATES), const(nb, S5_STATES), const(2, S5_STATES),
                  const(2, S5_HALF_IN, 2 * S5_HALF_ST), const(2, S5_HALF_ST, S5_HALF_IN),
                  const(2, S5_HALF_ST, S5_HALF_IN), const(1, A_WIDTH), const(A_WIDTH, A_WIDTH)],
        out_specs=[pl.BlockSpec((steps, nb, A_WIDTH), lambda j: (j, 0, 0)),
                   const(nb, S5_STATES), const(nb, S5_STATES)],
        out_shape=[jax.ShapeDtypeStruct((seq, nb, A_WIDTH), F32), jax.ShapeDtypeStruct((nb, S5_STATES), F32),
                   jax.ShapeDtypeStruct((nb, S5_STATES), F32)],
        scratch_shapes=[pltpu.VMEM((rows, 2 * S5_STATES), F32), pltpu.VMEM((nb, S5_STATES), F32),
                        pltpu.VMEM((nb, S5_STATES), F32)],
        compiler_params=_cparams(("arbitrary",)), name="s5_glu",
    )(u_tb, h0_re, h0_im, a, win, cre, cim, d_skip.reshape(1, A_WIDTH), w_glu_bf16)


GDN_BA_COLS = 128
HI = lax.Precision.HIGHEST


def _gdn_pre_kernel(x_ref, conv0_ref, ba_ref, w_ref, alog_ref, dtb_ref, q_ref, k_ref, v_ref, bg_ref, xp_sc,
                    *, tm, tmo):
    j = pl.program_id(1)

    @pl.when(j == 0)
    def _():
        xp_sc[0:8, :] = conv0_ref[0]

    @pl.when(j > 0)
    def _():
        xp_sc[0:8, :] = xp_sc[tm:tm + 8, :]

    xp_sc[8:8 + tm, :] = x_ref[0]
    conv = w_ref[0:1, :] * xp_sc[pl.ds(5, tm), :]
    for jj in range(1, C_CONV):
        conv = conv + w_ref[jj:jj + 1, :] * xp_sc[pl.ds(5 + jj, tm), :]
    act = conv * jax.nn.sigmoid(conv)
    pad = tmo - tm

    def put(ref, val):
        if pad:
            val = jnp.concatenate([val, jnp.zeros((pad, val.shape[1]), F32)], axis=0)
        ref[0] = val

    qs, ks = [], []
    for h in range(C_HEADS):
        qh = act[:, h * C_DK:(h + 1) * C_DK]
        kh = act[:, C_QK + h * C_DK:C_QK + (h + 1) * C_DK]
        qs.append(qh * lax.rsqrt(jnp.sum(qh * qh, axis=-1, keepdims=True) + EPS) * (C_DK ** -0.5))
        ks.append(kh * lax.rsqrt(jnp.sum(kh * kh, axis=-1, keepdims=True) + EPS))
    put(q_ref, jnp.concatenate(qs, axis=-1))
    put(k_ref, jnp.concatenate(ks, axis=-1))
    put(v_ref, act[:, 2 * C_QK:])
    ba = ba_ref[0]
    lane = lax.broadcasted_iota(jnp.int32, ba.shape, 1)
    beta = jax.nn.sigmoid(ba)
    g = -jnp.exp(alog_ref[...]) * jax.nn.softplus(ba + dtb_ref[...])
    bg = jnp.where(lane < C_HEADS, beta, jnp.where(lane < 2 * C_HEADS, g, 0.0))
    put(bg_ref, bg)


def _gdn_pre(proj3, conv0, conv_w, a_log, dt_bias, tm, tmo):
    bsz, seq, _ = proj3.shape
    nblk = seq // tm
    conv0_p = jnp.pad(conv0, ((0, 0), (8 - (C_CONV - 1), 0), (0, 0)))
    vec = lambda x: jnp.zeros((1, GDN_BA_COLS), F32).at[0, C_HEADS:2 * C_HEADS].set(x)
    ba_blk = (C_CONV_CH + C_V) // GDN_BA_COLS
    out_len = nblk * tmo
    return pl.pallas_call(
        functools.partial(_gdn_pre_kernel, tm=tm, tmo=tmo),
        grid=(bsz, nblk),
        in_specs=[pl.BlockSpec((1, tm, C_CONV_CH), lambda b, j: (b, j, 0)),
                  pl.BlockSpec((1, 8, C_CONV_CH), lambda b, j: (b, 0, 0)),
                  pl.BlockSpec((1, tm, GDN_BA_COLS), lambda b, j: (b, j, ba_blk)),
                  pl.BlockSpec((C_CONV, C_CONV_CH), lambda b, j: (0, 0)),
                  pl.BlockSpec((1, GDN_BA_COLS), lambda b, j: (0, 0)),
                  pl.BlockSpec((1, GDN_BA_COLS), lambda b, j: (0, 0))],
        out_specs=[pl.BlockSpec((1, tmo, C_QK), lambda b, j: (b, j, 0)),
                   pl.BlockSpec((1, tmo, C_QK), lambda b, j: (b, j, 0)),
                   pl.BlockSpec((1, tmo, C_V), lambda b, j: (b, j, 0)),
                   pl.BlockSpec((1, tmo, GDN_BA_COLS), lambda b, j: (b, j, 0))],
        out_shape=[jax.ShapeDtypeStruct((bsz, out_len, C_QK), F32), jax.ShapeDtypeStruct((bsz, out_len, C_QK), F32),
                   jax.ShapeDtypeStruct((bsz, out_len, C_V), F32),
                   jax.ShapeDtypeStruct((bsz, out_len, GDN_BA_COLS), F32)],
        scratch_shapes=[pltpu.VMEM((tm + 16, C_CONV_CH), F32)],
        compiler_params=_cparams(("parallel", "arbitrary")), name="gdn_pre",
    )(proj3, conv0_p, proj3, conv_w, vec(a_log), vec(dt_bias))


def _dot_hi(a, b):
    return jnp.dot(a, b, preferred_element_type=F32, precision=HI)


def _dot_nt_hi(a, b):
    return lax.dot_general(a, b, (((1,), (1,)), ((), ())), preferred_element_type=F32, precision=HI)


def _gdn_chunk_kernel(q_ref, k_ref, v_ref, bg_ref, z_ref, s0_ref, ng_ref, y_ref, s_out, s_sc, *, chunk, nchunk, zlen):
    j = pl.program_id(1)

    @pl.when(j == 0)
    def _():
        s_sc[...] = s0_ref[0]

    ri = lax.broadcasted_iota(jnp.int32, (chunk, chunk), 0)
    ci = lax.broadcasted_iota(jnp.int32, (chunk, chunk), 1)
    causal = ci <= ri
    strict = ci < ri
    ltri = causal.astype(F32)
    eye = (ci == ri).astype(F32)
    e0 = (lax.broadcasted_iota(jnp.int32, (chunk, C_DK), 1) == 0).astype(F32)
    n_dbl = chunk.bit_length() - 2

    def do_chunk(c, carry):
        r = pl.ds(pl.multiple_of(c * chunk, chunk), chunk)
        bg = bg_ref[0, r, :]
        for h in range(C_HEADS):
            q = q_ref[0, r, h * C_DK:(h + 1) * C_DK]
            k = k_ref[0, r, h * C_DK:(h + 1) * C_DK]
            v = v_ref[0, r, h * C_DV:(h + 1) * C_DV]
            beta = bg[:, h:h + 1]
            g = bg[:, C_HEADS + h:C_HEADS + h + 1]
            gcol = _dot_hi(ltri, jnp.broadcast_to(g, (chunk, C_DK)))
            grow = _dot_nt_hi(e0, gcol)
            decay = jnp.where(causal, jnp.exp(jnp.where(causal, gcol[:, :chunk] - grow, 0.0)), 0.0)
            kb = k * beta
            vb = v * beta
            m = jnp.where(strict, _dot_nt_hi(kb, k) * decay, 0.0)
            tinv = eye - m
            mp = m
            for _ in range(n_dbl):
                mp = _dot_hi(mp, mp)
                tinv = tinv + _dot_hi(tinv, mp)
            u = _dot_hi(tinv, vb)
            w = _dot_hi(tinv, kb * jnp.exp(gcol))
            s = s_sc[h]
            v_new = u - _dot_hi(w, s)
            a_intra = jnp.where(causal, _dot_nt_hi(q, k), 0.0) * decay
            o = _dot_hi(q * jnp.exp(gcol), s) + _dot_hi(a_intra, v_new)
            g_last = gcol[chunk - 1:chunk, :]
            k2 = k * jnp.exp(g_last - gcol)
            s_sc[h] = s * jnp.exp(g_last) + lax.dot_general(k2, v_new, (((0,), (0,)), ((), ())),
                                                            preferred_element_type=F32, precision=HI)
            o = o * lax.rsqrt(jnp.mean(o * o, axis=-1, keepdims=True) + EPS) * ng_ref[...]
            z = z_ref[0, pl.ds(pl.multiple_of(c * chunk, chunk), zlen), h * C_DV:(h + 1) * C_DV]
            y_ref[0, pl.ds(pl.multiple_of(c * chunk, chunk), zlen), h * C_DV:(h + 1) * C_DV] = (
                o[:zlen] * (z * jax.nn.sigmoid(z)))
        return carry

    lax.fori_loop(0, nchunk, do_chunk, 0)

    @pl.when(j == pl.num_programs(1) - 1)
    def _():
        s_out[0] = s_sc[...]


def _gdn_chunks(q, k, v, bg, proj3, s0, norm_g, chunk, nchunk, zlen):
    bsz, out_len, _ = q.shape
    rb = chunk * nchunk
    nblk = out_len // rb
    zrows = zlen if zlen < chunk else rb
    z_blk = C_CONV_CH // C_V
    row = lambda w: pl.BlockSpec((1, rb, w), lambda b, j: (b, j, 0))
    return pl.pallas_call(
        functools.partial(_gdn_chunk_kernel, chunk=chunk, nchunk=nchunk, zlen=min(zlen, chunk)),
        grid=(bsz, nblk),
        in_specs=[row(C_QK), row(C_QK), row(C_V), row(GDN_BA_COLS),
                  pl.BlockSpec((1, zrows, C_V), lambda b, j: (b, j, z_blk)),
                  pl.BlockSpec((1, C_HEADS, C_DK, C_DV), lambda b, j: (b, 0, 0, 0)),
                  pl.BlockSpec((1, C_DV), lambda b, j: (0, 0))],
        out_specs=[pl.BlockSpec((1, zrows, C_V), lambda b, j: (b, j, 0)),
                   pl.BlockSpec((1, C_HEADS, C_DK, C_DV), lambda b, j: (b, 0, 0, 0))],
        out_shape=[jax.ShapeDtypeStruct((bsz, nblk * zrows, C_V), F32),
                   jax.ShapeDtypeStruct((bsz, C_HEADS, C_DK, C_DV), F32)],
        scratch_shapes=[pltpu.VMEM((C_HEADS, C_DK, C_DV), F32)],
        compiler_params=_cparams(("parallel", "arbitrary")), name="gdn_chunks",
    )(q, k, v, bg, proj3, s0, norm_g.reshape(1, C_DV))


def _tri_pairs(n):
    qi = [i for i in range(n) for _ in range(i + 1)]
    kj = [j for i in range(n) for j in range(i + 1)]
    return jnp.asarray(qi, jnp.int32), jnp.asarray(kj, jnp.int32)


def _softmax_update(s, v_bf16, m_ref, l_ref, acc_ref, idx):
    m_prev = m_ref[idx]
    m_new = jnp.maximum(m_prev, jnp.max(s, axis=-1, keepdims=True))
    corr = jnp.exp(m_prev - m_new)
    p = jnp.exp(s - m_new)
    l_ref[idx] = corr * l_ref[idx] + jnp.sum(p, axis=-1, keepdims=True)
    acc_ref[idx] = corr * acc_ref[idx] + jnp.dot(p.astype(BF16), v_bf16, preferred_element_type=F32)
    m_ref[idx] = m_new


def _diff_attn_kernel(qi_ref, kj_ref, lam_ref, q_ref, k_ref, v_ref, g_ref, o_ref, m_ref, l_ref, acc_ref,
                      *, blk, out_scale):
    p = pl.program_id(2)
    qi = qi_ref[p]
    kj = kj_ref[p]

    @pl.when(kj == 0)
    def _():
        m_ref[...] = jnp.full(m_ref.shape, NEG_BIG, F32)
        l_ref[...] = jnp.zeros(l_ref.shape, F32)
        acc_ref[...] = jnp.zeros(acc_ref.shape, F32)

    def step(masked):
        q = (q_ref[...] * (B_DK ** -0.5)).astype(BF16)
        k = k_ref[...].astype(BF16)
        v = v_ref[...].astype(BF16)
        for c in range(2):
            s = lax.dot_general(q[:, c * B_DK:(c + 1) * B_DK], k[:, c * B_DK:(c + 1) * B_DK],
                                (((1,), (1,)), ((), ())), preferred_element_type=F32)
            if masked:
                row = lax.broadcasted_iota(jnp.int32, s.shape, 0)
                col = lax.broadcasted_iota(jnp.int32, s.shape, 1)
                s = jnp.where(col <= row, s, NEG_BIG)
            _softmax_update(s, v, m_ref, l_ref, acc_ref, c)

    @pl.when(kj < qi)
    def _():
        step(False)

    @pl.when(kj == qi)
    def _():
        step(True)
        lam = lam_ref[0, 0]
        o0 = acc_ref[0] / l_ref[0]
        o1 = acc_ref[1] / l_ref[1]
        y = o0 - lam * o1
        y = y * lax.rsqrt(jnp.mean(y * y, axis=-1, keepdims=True) + EPS)
        o_ref[...] = y * g_ref[...] * out_scale


def _diff_attn_prompt(q, k, v, lam, subln, bsz, seq, lam_init, blk=512):
    nq = seq // blk
    qi, kj = _tri_pairs(nq)
    kern = functools.partial(_diff_attn_kernel, blk=blk, out_scale=1.0 - lam_init)
    grid_spec = pltpu.PrefetchScalarGridSpec(
        num_scalar_prefetch=2,
        grid=(bsz, B_HEADS, qi.shape[0]),
        in_specs=[
            pl.BlockSpec(memory_space=pltpu.SMEM),
            pl.BlockSpec((blk, B_DV), lambda b, h, p, qi, kj: (b * nq + qi[p], h)),
            pl.BlockSpec((blk, B_DV), lambda b, h, p, qi, kj: (b * nq + kj[p], h)),
            pl.BlockSpec((blk, B_DV), lambda b, h, p, qi, kj: (b * nq + kj[p], h)),
            pl.BlockSpec((1, B_DV), lambda b, h, p, qi, kj: (0, 0)),
        ],
        out_specs=pl.BlockSpec((blk, B_DV), lambda b, h, p, qi, kj: (b * nq + qi[p], h)),
        scratch_shapes=[pltpu.VMEM((2, blk, 1), F32), pltpu.VMEM((2, blk, 1), F32),
                        pltpu.VMEM((2, blk, B_DV), F32)],
    )
    return pl.pallas_call(
        kern, grid_spec=grid_spec, out_shape=jax.ShapeDtypeStruct(q.shape, F32),
        compiler_params=_cparams(("parallel", "parallel", "arbitrary")), name="diff_attn_prompt",
    )(qi, kj, lam.reshape(1, 1), q, k, v, subln.reshape(1, B_DV))


def _mla_attn_kernel(qi_ref, kj_ref, ql_ref, qr_ref, c_ref, kr_ref, wuv_ref, o_ref, m_ref, l_ref, acc_ref, *, blk):
    p = pl.program_id(1)
    qi = qi_ref[p]
    kj = kj_ref[p]
    scale = (D_NOPE + D_ROPE) ** -0.5

    @pl.when(kj == 0)
    def _():
        m_ref[...] = jnp.full(m_ref.shape, NEG_BIG, F32)
        l_ref[...] = jnp.zeros(l_ref.shape, F32)
        acc_ref[...] = jnp.zeros(acc_ref.shape, F32)

    def step(masked):
        c = c_ref[...].astype(BF16)
        kr = kr_ref[...].astype(BF16)
        for h in range(D_HEADS):
            ql = (ql_ref[:, h * D_KV_LORA:(h + 1) * D_KV_LORA] * scale).astype(BF16)
            qr = (qr_ref[:, h * D_ROPE:(h + 1) * D_ROPE] * scale).astype(BF16)
            s = (lax.dot_general(ql, c, (((1,), (1,)), ((), ())), preferred_element_type=F32)
                 + lax.dot_general(qr, kr, (((1,), (1,)), ((), ())), preferred_element_type=F32))
            if masked:
                row = lax.broadcasted_iota(jnp.int32, s.shape, 0)
                col = lax.broadcasted_iota(jnp.int32, s.shape, 1)
                s = jnp.where(col <= row, s, NEG_BIG)
            _softmax_update(s, c, m_ref, l_ref, acc_ref, h)

    @pl.when(kj < qi)
    def _():
        step(False)

    @pl.when(kj == qi)
    def _():
        step(True)
        for h in range(D_HEADS):
            o = (acc_ref[h] / l_ref[h]).astype(BF16)
            o_ref[:, h * D_DV:(h + 1) * D_DV] = jnp.dot(o, wuv_ref[h], preferred_element_type=F32)


def _mla_attn_prompt(q_lat, q_rope, c, kr, w_uv_bf16, bsz, seq, blk=512):
    nq = seq // blk
    qi, kj = _tri_pairs(nq)
    grid_spec = pltpu.PrefetchScalarGridSpec(
        num_scalar_prefetch=2,
        grid=(bsz, qi.shape[0]),
        in_specs=[
            pl.BlockSpec((blk, D_HEADS * D_KV_LORA), lambda b, p, qi, kj: (b * nq + qi[p], 0)),
            pl.BlockSpec((blk, D_HEADS * D_ROPE), lambda b, p, qi, kj: (b * nq + qi[p], 0)),
            pl.BlockSpec((blk, D_KV_LORA), lambda b, p, qi, kj: (b * nq + kj[p], 0)),
            pl.BlockSpec((blk, D_ROPE), lambda b, p, qi, kj: (b * nq + kj[p], 0)),
            pl.BlockSpec((D_HEADS, D_KV_LORA, D_DV), lambda b, p, qi, kj: (0, 0, 0)),
        ],
        out_specs=pl.BlockSpec((blk, D_HEADS * D_DV), lambda b, p, qi, kj: (b * nq + qi[p], 0)),
        scratch_shapes=[pltpu.VMEM((D_HEADS, blk, 1), F32), pltpu.VMEM((D_HEADS, blk, 1), F32),
                        pltpu.VMEM((D_HEADS, blk, D_KV_LORA), F32)],
    )
    return pl.pallas_call(
        functools.partial(_mla_attn_kernel, blk=blk), grid_spec=grid_spec,
        out_shape=jax.ShapeDtypeStruct((q_lat.shape[0], D_HEADS * D_DV), F32),
        compiler_params=_cparams(("parallel", "arbitrary")), name="mla_attn_prompt",
    )(qi, kj, q_lat, q_rope, c, kr, w_uv_bf16)


def _moe_kernel(be_ref, nu_ref, x_ref, wgu_ref, bgu_ref, wd_ref, bd_ref, o_ref, wgu_bf, wd_bf):
    i = pl.program_id(0)
    e = be_ref[i]
    e_prev = be_ref[jnp.maximum(i - 1, 0)]

    @pl.when((i == 0) | (e != e_prev))
    def _():
        wgu_bf[...] = wgu_ref[0].astype(BF16)
        wd_bf[...] = wd_ref[0].astype(BF16)

    @pl.when(i < nu_ref[0])
    def _():
        h = jnp.dot(x_ref[...].astype(BF16), wgu_bf[...], preferred_element_type=F32) + bgu_ref[0]
        gate = jnp.minimum(h[:, :D_FF], SWIGLU_LIMIT)
        up = jnp.clip(h[:, D_FF:], -SWIGLU_LIMIT, SWIGLU_LIMIT)
        act = (up + 1.0) * gate * jax.nn.sigmoid(SWIGLU_ALPHA * gate)
        o_ref[...] = jnp.dot(act.astype(BF16), wd_bf[...], preferred_element_type=F32) + bd_ref[0]

    @pl.when(i >= nu_ref[0])
    def _():
        o_ref[...] = jnp.zeros(o_ref.shape, F32)


def _moe_experts(x_sorted, block_e, n_used, w_gu, b_gu, w_down, b_down, blk):
    p_rows, d = x_sorted.shape
    n_blocks = p_rows // blk
    grid_spec = pltpu.PrefetchScalarGridSpec(
        num_scalar_prefetch=2,
        grid=(n_blocks,),
        in_specs=[
            pl.BlockSpec((blk, d), lambda i, be, nu: (i, 0)),
            pl.BlockSpec((1, d, 2 * D_FF), lambda i, be, nu: (be[i], 0, 0)),
            pl.BlockSpec((1, 1, 2 * D_FF), lambda i, be, nu: (be[i], 0, 0)),
            pl.BlockSpec((1, D_FF, d), lambda i, be, nu: (be[i], 0, 0)),
            pl.BlockSpec((1, 1, d), lambda i, be, nu: (be[i], 0, 0)),
        ],
        out_specs=pl.BlockSpec((blk, d), lambda i, be, nu: (i, 0)),
        scratch_shapes=[pltpu.VMEM((d, 2 * D_FF), BF16), pltpu.VMEM((D_FF, d), BF16)],
    )
    return pl.pallas_call(
        _moe_kernel, grid_spec=grid_spec, out_shape=jax.ShapeDtypeStruct((p_rows, d), F32),
        compiler_params=_cparams(("arbitrary",)), name="moe_experts",
    )(block_e, n_used, x_sorted, w_gu, b_gu.reshape(N_EXPERTS, 1, 2 * D_FF), w_down,
      b_down.reshape(N_EXPERTS, 1, d))


def _moe(xt, router_w, router_b, w_gu, b_gu, w_down, b_down):
    n, d = xt.shape
    blk = 256 if n >= 8192 else 128
    logits = jnp.dot(xt, router_w, precision=lax.Precision.HIGHEST) + router_b
    top_val, top_idx = lax.top_k(logits, TOP_K)
    gates = jax.nn.softmax(top_val, axis=-1)
    nk = n * TOP_K
    flat_e = top_idx.reshape(-1).astype(jnp.int32)
    onehot = (flat_e[:, None] == jnp.arange(N_EXPERTS, dtype=jnp.int32)[None, :]).astype(jnp.int32)
    csum = jnp.cumsum(onehot, axis=0)
    counts = csum[-1]
    rank = jnp.sum((csum - onehot) * onehot, axis=1)
    padded = (counts + blk - 1) // blk * blk
    pend = jnp.cumsum(padded)
    pstart = pend - padded
    dest = pstart[flat_e] + rank
    n_blocks = -(-(nk + N_EXPERTS * (blk - 1)) // blk)
    p_rows = n_blocks * blk
    buf_tok = jnp.full((p_rows,), n, jnp.int32).at[dest].set(jnp.arange(nk, dtype=jnp.int32) // TOP_K,
                                                               unique_indices=True)
    block_e = jnp.minimum(jnp.searchsorted(pend, jnp.arange(n_blocks, dtype=jnp.int32) * blk, side='right'),
                          N_EXPERTS - 1).astype(jnp.int32)
    n_used = (pend[-1] // blk).astype(jnp.int32).reshape(1)
    x_pad = jnp.concatenate([xt, jnp.zeros((1, d), xt.dtype)], axis=0)
    x_sorted = x_pad[buf_tok]
    y_buf = _moe_experts(x_sorted, block_e, n_used, w_gu, b_gu, w_down, b_down, blk)
    y_assign = y_buf[dest].reshape(n, TOP_K, d)
    return jnp.einsum('nkd,nk->nd', y_assign, gates)


def _rms_norm(x, g):
    return x * lax.rsqrt(jnp.mean(x * x, axis=-1, keepdims=True) + EPS) * g


def _layer_norm(x, g, b):
    mu = jnp.mean(x, axis=-1, keepdims=True)
    var = jnp.mean(jnp.square(x - mu), axis=-1, keepdims=True)
    return (x - mu) * lax.rsqrt(var + EPS) * g + b


def _rope(x, pos, theta, n_rot):
    half = n_rot // 2
    inv = jnp.exp(-math.log(theta) * jnp.arange(half, dtype=F32) * (2.0 / n_rot))
    ang = pos[:, None] * inv[None, :]
    cos = jnp.cos(ang)[:, None, :]
    sin = jnp.sin(ang)[:, None, :]
    x1 = x[..., :half]
    x2 = x[..., half:n_rot]
    return jnp.concatenate([x1 * cos - x2 * sin, x2 * cos + x1 * sin, x[..., n_rot:]], axis=-1)


def _online_softmax_step(carry, s, vals, pv):
    m, l, acc = carry
    m_new = jnp.maximum(m, jnp.max(s, axis=-1))
    corr = jnp.exp(m - m_new)
    p = jnp.exp(s - m_new[..., None])
    return (m_new, l * corr + jnp.sum(p, axis=-1), acc * corr[..., None] + pv(p, vals))


def _diff_attn_sample(q, k, v, cache_k, cache_v, layer, page_table):
    bd, T = q.shape[:2]
    scale = B_DK ** -0.5

    def pv(p, vals):
        return jnp.einsum('bhcts,bshd->bhctd', p, vals)

    def page_step(carry, pages):
        kp = cache_k[layer, pages].reshape(bd, PAGE_SIZE, B_HEADS, 2, B_DK)
        vp = cache_v[layer, pages]
        s = jnp.einsum('bthcd,bshcd->bhcts', q, kp, preferred_element_type=F32) * scale
        return _online_softmax_step(carry, s, vp, pv), None

    init = (jnp.full((bd, B_HEADS, 2, T), NEG_BIG, F32), jnp.zeros((bd, B_HEADS, 2, T), F32),
            jnp.zeros((bd, B_HEADS, 2, T, B_DV), F32))
    carry, _ = lax.scan(page_step, init, page_table.T)
    tq = jnp.arange(T)
    s_self = jnp.einsum('bthcd,bshcd->bhcts', q, k, preferred_element_type=F32) * scale
    s_self = jnp.where(tq[None, :] <= tq[:, None], s_self, -jnp.inf)
    _, l, acc = _online_softmax_step(carry, s_self, v, pv)
    o = acc / l[..., None]
    return o.transpose(0, 3, 1, 2, 4)


def _even_mixer(x, pos, h0_re, h0_im, sample_ctx, lam_init, w_in, lam_re, lam_im, log_dt, b_re, b_im,
                c_re, c_im, d_skip, w_glu, lq1, lk1, lq2, lk2, subln, w_out):
    bsz, L, _ = x.shape
    n = bsz * L
    proj = _mm(x.reshape(n, D_MODEL), w_in.astype(BF16))
    u_tb = proj[:, :A_WIDTH].reshape(bsz, L, A_WIDTH).transpose(1, 0, 2)
    q = proj[:, A_WIDTH:A_WIDTH + B_QK]
    k = proj[:, A_WIDTH + B_QK:A_WIDTH + 2 * B_QK]
    v = proj[:, A_WIDTH + 2 * B_QK:]
    s5w = _s5_weights(lam_re, lam_im, log_dt, b_re, b_im, c_re, c_im)
    y_tb, h_re, h_im = _s5_glu(u_tb, h0_re.reshape(bsz, S5_STATES), h0_im.reshape(bsz, S5_STATES), s5w,
                               d_skip, w_glu.astype(BF16), steps=min(L, 16))
    y_a = y_tb.transpose(1, 0, 2).reshape(n, A_WIDTH)
    h_re = h_re.reshape(bsz, A_GROUPS, A_STATE)
    h_im = h_im.reshape(bsz, A_GROUPS, A_STATE)
    q = _rope(q.reshape(bsz, L, 2 * B_HEADS, B_DK), pos, ROPE_THETA, B_ROT)
    k = _rope(k.reshape(bsz, L, 2 * B_HEADS, B_DK), pos, ROPE_THETA, B_ROT)
    lam = jnp.exp(jnp.sum(lq1 * lk1)) - jnp.exp(jnp.sum(lq2 * lk2)) + lam_init
    if sample_ctx is None:
        y_b = _diff_attn_prompt(q.reshape(n, B_QK), k.reshape(n, B_QK), v, lam, subln, bsz, L, lam_init)
    else:
        cache_k, cache_v, layer, page_table = sample_ctx
        o = _diff_attn_sample(q.reshape(bsz, L, B_HEADS, 2, B_DK), k.reshape(bsz, L, B_HEADS, 2, B_DK),
                              v.reshape(bsz, L, B_HEADS, B_DV), cache_k, cache_v, layer, page_table)
        y_b = o[..., 0, :] - lam * o[..., 1, :]
        y_b = (_rms_norm(y_b, subln) * (1.0 - lam_init)).reshape(n, B_HEADS * B_DV)
    y = _mm(jnp.concatenate([y_a, y_b], axis=-1), w_out.astype(BF16))
    return (y.reshape(bsz, L, D_MODEL), h_re, h_im, k.reshape(bsz, L, B_HEADS, 2 * B_DK),
            v.reshape(bsz, L, B_HEADS, B_DV))


def _mla_attn_sample(q_lat, q_rope, c, kr, cache_lat, cache_rope, layer, page_table):
    bd, T = q_lat.shape[:2]
    scale = (D_NOPE + D_ROPE) ** -0.5

    def pv(p, vals):
        return jnp.einsum('bhts,bsr->bhtr', p, vals)

    def scores(cc, rr):
        return (jnp.einsum('bthr,bsr->bhts', q_lat, cc, preferred_element_type=F32)
                + jnp.einsum('bthd,bsd->bhts', q_rope, rr, preferred_element_type=F32)) * scale

    def page_step(carry, pages):
        cc = cache_lat[layer, pages]
        rr = cache_rope[layer, pages]
        return _online_softmax_step(carry, scores(cc, rr), cc, pv), None

    init = (jnp.full((bd, D_HEADS, T), NEG_BIG, F32), jnp.zeros((bd, D_HEADS, T), F32),
            jnp.zeros((bd, D_HEADS, T, D_KV_LORA), F32))
    carry, _ = lax.scan(page_step, init, page_table.T)
    tq = jnp.arange(T)
    s_self = jnp.where(tq[None, :] <= tq[:, None], scores(c, kr), -jnp.inf)
    _, l, acc = _online_softmax_step(carry, s_self, c, pv)
    return (acc / l[..., None]).transpose(0, 2, 1, 3)


def _odd_mixer(x, pos, conv0, s0, sample_ctx, w_in, conv_w, a_log, dt_bias, gdn_g,
               q_norm, w_uq, kv_norm, w_uk, w_uv, w_out):
    bsz, L, _ = x.shape
    n = bsz * L
    zcols = lambda w: jnp.zeros((D_MODEL, w), F32)
    ba_end = C_CONV_CH + C_V + 2 * C_HEADS
    w_in_p = jnp.concatenate([w_in[:, :ba_end], zcols(GDN_BA_COLS - 2 * C_HEADS), w_in[:, ba_end:],
                              zcols(128 - D_ROPE)], axis=1).astype(BF16)
    o_cq = C_CONV_CH + C_V + GDN_BA_COLS
    o_ckv = o_cq + D_Q_LORA
    o_kr = o_ckv + D_KV_LORA
    proj = _mm(x.reshape(n, D_MODEL), w_in_p)
    proj3 = proj.reshape(bsz, L, proj.shape[1])
    if L >= C_CHUNK:
        tm, tmo, chunk, nchunk, zlen = 512, 512, C_CHUNK, 512 // C_CHUNK, C_CHUNK
    else:
        tm, tmo, chunk, nchunk, zlen = L, 8, 8, 1, L
    gq, gk, gv, gbg = _gdn_pre(proj3, conv0, conv_w, a_log, dt_bias, tm, tmo)
    y_c, S = _gdn_chunks(gq, gk, gv, gbg, proj3, s0, gdn_g, chunk, nchunk, zlen)
    conv_new = proj3[:, L - (C_CONV - 1):, :C_CONV_CH]
    c_q = proj3[..., o_cq:o_ckv]
    c_kv = proj3[..., o_ckv:o_kr]
    k_r = proj3[..., o_kr:o_kr + D_ROPE]
    qf = _mm(_rms_norm(c_q, q_norm).reshape(n, D_Q_LORA), w_uq.astype(BF16))
    qf = qf.reshape(bsz, L, D_HEADS, D_NOPE + D_ROPE)
    q_rope = _rope(qf[..., D_NOPE:], pos, MLA_ROPE_THETA, D_ROPE)
    q_nope = qf[..., :D_NOPE].reshape(n, D_HEADS * D_NOPE)
    w_bd = jnp.zeros((D_HEADS * D_NOPE, D_HEADS * D_KV_LORA), F32)
    for h in range(D_HEADS):
        w_bd = w_bd.at[h * D_NOPE:(h + 1) * D_NOPE, h * D_KV_LORA:(h + 1) * D_KV_LORA].set(w_uk[h])
    q_lat = _mm(q_nope, w_bd.astype(BF16))
    c = _rms_norm(c_kv, kv_norm)
    kr = _rope(k_r[:, :, None, :], pos, MLA_ROPE_THETA, D_ROPE)[:, :, 0, :]
    if sample_ctx is None:
        y_d = _mla_attn_prompt(q_lat, q_rope.reshape(n, D_HEADS * D_ROPE), c.reshape(n, D_KV_LORA),
                               kr.reshape(n, D_ROPE), w_uv.astype(BF16), bsz, L)
    else:
        cache_lat, cache_rope, layer, page_table = sample_ctx
        o_lat = _mla_attn_sample(q_lat.reshape(bsz, L, D_HEADS, D_KV_LORA), q_rope, c, kr, cache_lat, cache_rope,
                                 layer, page_table)
        w_uv_bd = jnp.zeros((D_HEADS * D_KV_LORA, D_HEADS * D_DV), F32)
        for h in range(D_HEADS):
            w_uv_bd = w_uv_bd.at[h * D_KV_LORA:(h + 1) * D_KV_LORA, h * D_DV:(h + 1) * D_DV].set(w_uv[h])
        y_d = _mm(o_lat.reshape(n, D_HEADS * D_KV_LORA), w_uv_bd.astype(BF16))
    y = _mm(jnp.concatenate([y_c.reshape(n, C_V), y_d], axis=-1), w_out.astype(BF16))
    return y.reshape(bsz, L, D_MODEL), S, conv_new, c, kr


def _post_block(h, mix, li, router_w, router_b, moe_w_gu, moe_b_gu, moe_w_down, moe_b_down, ln_g, ln_b):
    bsz, L, d = h.shape
    h = _layer_norm(DN_ALPHA * h + mix, ln_g[li, 0], ln_b[li, 0])
    f = _moe(h.reshape(-1, d), router_w[li], router_b[li], moe_w_gu[li], moe_b_gu[li], moe_w_down[li],
             moe_b_down[li]).reshape(bsz, L, d)
    return _layer_norm(DN_ALPHA * h + f, ln_g[li, 1], ln_b[li, 1])


def kernel(x_prompt, x_sample, state_a_re, state_a_im, cache_b_k, cache_b_v, state_c, state_c_conv,
           cache_d_latent, cache_d_rope, page_table, w_in_even, s5_lam_re, s5_lam_im, s5_log_dt,
           s5_b_re, s5_b_im, s5_c_re, s5_c_im, s5_d, s5_w_glu, diff_lq1, diff_lk1, diff_lq2, diff_lk2,
           diff_subln, w_out_even, w_in_odd, gdn_conv_w, gdn_a_log, gdn_dt_bias, gdn_norm, mla_q_norm,
           mla_w_uq, mla_kv_norm, mla_w_uk, mla_w_uv, w_out_odd, router_w, router_b, moe_w_gu, moe_b_gu,
           moe_w_down, moe_b_down, ln_g, ln_b):
    bp, lp, _ = x_prompt.shape
    bs, ls, _ = x_sample.shape
    pos_p = jnp.arange(lp, dtype=F32)
    pos_s = PAST_LEN + jnp.arange(ls, dtype=F32)
    hp, hs = x_prompt, x_sample
    outs_p = {k: [] for k in ('a_re', 'a_im', 'b_k', 'b_v', 'c', 'c_conv', 'd_lat', 'd_rope')}
    outs_s = {k: [] for k in outs_p}
    for li in range(DEPTH):
        if li % 2 == 0:
            e = li // 2
            lam_init = 0.8 - 0.6 * math.exp(-0.3 * li)
            ew = (w_in_even[e], s5_lam_re[e], s5_lam_im[e], s5_log_dt[e], s5_b_re[e], s5_b_im[e],
                  s5_c_re[e], s5_c_im[e], s5_d[e], s5_w_glu[e], diff_lq1[e], diff_lk1[e], diff_lq2[e],
                  diff_lk2[e], diff_subln[e], w_out_even[e])
            z0 = jnp.zeros((bp, A_GROUPS, A_STATE), F32)
            mp, hr, hi, kk, vv = _even_mixer(hp, pos_p, z0, z0, None, lam_init, *ew)
            for key, val in zip(('a_re', 'a_im', 'b_k', 'b_v'), (hr, hi, kk, vv)):
                outs_p[key].append(val)
            ctx = (cache_b_k, cache_b_v, e, page_table)
            ms, hr, hi, kk, vv = _even_mixer(hs, pos_s, state_a_re[e], state_a_im[e], ctx, lam_init, *ew)
            for key, val in zip(('a_re', 'a_im', 'b_k', 'b_v'), (hr, hi, kk, vv)):
                outs_s[key].append(val)
        else:
            o = li // 2
            ow = (w_in_odd[o], gdn_conv_w[o], gdn_a_log[o], gdn_dt_bias[o], gdn_norm[o], mla_q_norm[o],
                  mla_w_uq[o], mla_kv_norm[o], mla_w_uk[o], mla_w_uv[o], w_out_odd[o])
            conv0 = jnp.zeros((bp, C_CONV - 1, C_CONV_CH), F32)
            s0 = jnp.zeros((bp, C_HEADS, C_DK, C_DV), F32)
            mp, S, cb, cl, kr = _odd_mixer(hp, pos_p, conv0, s0, None, *ow)
            for key, val in zip(('c', 'c_conv', 'd_lat', 'd_rope'), (S, cb, cl, kr)):
                outs_p[key].append(val)
            ctx = (cache_d_latent, cache_d_rope, o, page_table)
            ms, S, cb, cl, kr = _odd_mixer(hs, pos_s, state_c_conv[o], state_c[o], ctx, *ow)
            for key, val in zip(('c', 'c_conv', 'd_lat', 'd_rope'), (S, cb, cl, kr)):
                outs_s[key].append(val)
        post = (li, router_w, router_b, moe_w_gu, moe_b_gu, moe_w_down, moe_b_down, ln_g, ln_b)
        hp = _post_block(hp, mp, *post)
        hs = _post_block(hs, ms, *post)
    keys = ('a_re', 'a_im', 'b_k', 'b_v', 'c', 'c_conv', 'd_lat', 'd_rope')
    return ((hp, hs) + tuple(jnp.stack(outs_p[k]) for k in keys) + tuple(jnp.stack(outs_s[k]) for k in keys))
```

```python
import functools
import math

import jax
import jax.numpy as jnp
from jax import lax
from jax.experimental import pallas as pl
from jax.experimental.pallas import tpu as pltpu

D_MODEL = 1024
DEPTH = 2
PAST_LEN = 16384
PAGE_SIZE = 128
A_WIDTH = D_MODEL // 2
A_GROUP = 16
A_GROUPS = A_WIDTH // A_GROUP
A_STATE = 64
B_HEADS = 4
B_DK = 64
B_DV = 2 * B_DK
B_ROT = B_DK // 4
B_QK = B_HEADS * 2 * B_DK
C_HEADS = 4
C_DK = 128
C_DV = 128
C_CONV = 4
C_CHUNK = 64
C_QK = C_HEADS * C_DK
C_V = C_HEADS * C_DV
C_CONV_CH = 2 * C_QK + C_V
C_IN = C_CONV_CH + C_V + 2 * C_HEADS
D_HEADS = 4
D_NOPE = 128
D_ROPE = 32
D_DV = 128
D_Q_LORA = 384
D_KV_LORA = 256
ROPE_THETA = 500000.0
MLA_ROPE_THETA = 10000.0
N_EXPERTS = 32
TOP_K = 4
D_FF = 1024
SWIGLU_LIMIT = 7.0
SWIGLU_ALPHA = 1.702
DN_ALPHA = (2 * DEPTH) ** 0.25
EPS = 1e-6
NEG_BIG = -1e30

V7X_VMEM_BYTES = 64 * 1024 * 1024
VMEM_LIMIT = V7X_VMEM_BYTES * 7 // 8
BF16 = jnp.bfloat16
F32 = jnp.float32


def _cparams(sem):
    return pltpu.CompilerParams(dimension_semantics=sem, vmem_limit_bytes=VMEM_LIMIT)


def _mm_kernel(x_ref, w_ref, o_ref):
    o_ref[...] = jnp.dot(x_ref[...].astype(BF16), w_ref[...], preferred_element_type=F32)


def _mm(x, w_bf16, tm=256):
    m, k = x.shape
    n = w_bf16.shape[1]
    tm = min(tm, m)
    return pl.pallas_call(
        _mm_kernel,
        grid=(m // tm,),
        in_specs=[pl.BlockSpec((tm, k), lambda i: (i, 0)), pl.BlockSpec((k, n), lambda i: (0, 0))],
        out_specs=pl.BlockSpec((tm, n), lambda i: (i, 0)),
        out_shape=jax.ShapeDtypeStruct((m, n), F32),
        compiler_params=_cparams(("parallel",)),
        name="mm",
    )(x, w_bf16)


S5_HALF_IN = A_WIDTH // 2
S5_HALF_ST = A_GROUPS * A_STATE // 2
S5_STATES = A_GROUPS * A_STATE


def _s5_kernel(u_ref, h0r_ref, h0i_ref, a_ref, win_ref, cre_ref, cim_ref, d_ref, wglu_ref,
               y_ref, hr_out, hi_out, x_sc, hr_sc, hi_sc, *, nb, steps):
    j = pl.program_id(0)

    @pl.when(j == 0)
    def _():
        hr_sc[...] = h0r_ref[...]
        hi_sc[...] = h0i_ref[...]

    rows = steps * nb
    u = u_ref[...].reshape(rows, A_WIDTH)
    ub = u.astype(BF16)
    for k in range(2):
        xk = jnp.dot(ub[:, k * S5_HALF_IN:(k + 1) * S5_HALF_IN], win_ref[k], preferred_element_type=F32)
        x_sc[:, k * S5_HALF_ST:(k + 1) * S5_HALF_ST] = xk[:, :S5_HALF_ST]
        x_sc[:, S5_STATES + k * S5_HALF_ST:S5_STATES + (k + 1) * S5_HALF_ST] = xk[:, S5_HALF_ST:]
    ar = a_ref[0:1, :]
    ai = a_ref[1:2, :]

    def body(t, carry):
        r = pl.ds(pl.multiple_of(t * nb, nb), nb)
        hr = hr_sc[...]
        hi = hi_sc[...]
        nr = ar * hr - ai * hi + x_sc[r, 0:S5_STATES]
        ni = ar * hi + ai * hr + x_sc[r, S5_STATES:2 * S5_STATES]
        x_sc[r, 0:S5_STATES] = nr
        x_sc[r, S5_STATES:2 * S5_STATES] = ni
        hr_sc[...] = nr
        hi_sc[...] = ni
        return carry

    lax.fori_loop(0, steps, body, 0)
    ys = []
    for k in range(2):
        hk_r = x_sc[:, k * S5_HALF_ST:(k + 1) * S5_HALF_ST].astype(BF16)
        hk_i = x_sc[:, S5_STATES + k * S5_HALF_ST:S5_STATES + (k + 1) * S5_HALF_ST].astype(BF16)
        yk = (jnp.dot(hk_r, cre_ref[k], preferred_element_type=F32)
              - jnp.dot(hk_i, cim_ref[k], preferred_element_type=F32))
        ys.append(yk)
    y = jnp.concatenate(ys, axis=-1) + d_ref[...] * u
    y = jax.nn.gelu(y)
    y = y * jax.nn.sigmoid(jnp.dot(y.astype(BF16), wglu_ref[...], preferred_element_type=F32))
    y_ref[...] = y.reshape(steps, nb, A_WIDTH)

    @pl.when(j == pl.num_programs(0) - 1)
    def _():
        hr_out[...] = hr_sc[...]
        hi_out[...] = hi_sc[...]


def _s5_weights(lam_re, lam_im, log_dt, b_re, b_im, c_re, c_im):
    dt = jnp.exp(log_dt)[:, None]
    mag = jnp.exp(lam_re * dt)
    ar = mag * jnp.cos(lam_im * dt)
    ai = mag * jnp.sin(lam_im * dt)
    den = lam_re * lam_re + lam_im * lam_im
    fr = ((ar - 1.0) * lam_re + ai * lam_im) / den
    fi = (ai * lam_re - (ar - 1.0) * lam_im) / den
    bbr = fr[..., None] * b_re - fi[..., None] * b_im
    bbi = fr[..., None] * b_im + fi[..., None] * b_re
    eye = jnp.eye(A_GROUPS, dtype=F32)

    def in_map(bb):
        return jnp.einsum('gh,gnp->gphn', eye, bb).reshape(A_WIDTH, S5_STATES)

    def out_map(c):
        return jnp.einsum('gh,gpn->gnhp', eye, c).reshape(S5_STATES, A_WIDTH)

    wr, wi = in_map(bbr), in_map(bbi)
    win = jnp.stack([jnp.concatenate([w[k * S5_HALF_IN:(k + 1) * S5_HALF_IN, k * S5_HALF_ST:(k + 1) * S5_HALF_ST]
                                      for w in (wr, wi)], axis=1) for k in range(2)]).astype(BF16)
    cr, ci = out_map(c_re), out_map(c_im)
    cre = jnp.stack([cr[k * S5_HALF_ST:(k + 1) * S5_HALF_ST, k * S5_HALF_IN:(k + 1) * S5_HALF_IN]
                     for k in range(2)]).astype(BF16)
    cim = jnp.stack([ci[k * S5_HALF_ST:(k + 1) * S5_HALF_ST, k * S5_HALF_IN:(k + 1) * S5_HALF_IN]
                     for k in range(2)]).astype(BF16)
    a = jnp.stack([ar.reshape(-1), ai.reshape(-1)])
    return a, win, cre, cim


def _s5_glu(u_tb, h0_re, h0_im, weights, d_skip, w_glu_bf16, steps):
    seq, nb, _ = u_tb.shape
    a, win, cre, cim = weights
    rows = steps * nb
    const = lambda *shape: pl.BlockSpec(shape, lambda j: (0,) * len(shape))
    return pl.pallas_call(
        functools.partial(_s5_kernel, nb=nb, steps=steps),
        grid=(seq // steps,),
        in_specs=[pl.BlockSpec((steps, nb, A_WIDTH), lambda j: (j, 0, 0)),
                  const(nb, S5_STATES), const(nb, S5_STATES), const(2, S5_STATES),
                  const(2, S5_HALF_IN, 2 * S5_HALF_ST), const(2, S5_HALF_ST, S5_HALF_IN),
                  const(2, S5_HALF_ST, S5_HALF_IN), const(1, A_WIDTH), const(A_WIDTH, A_WIDTH)],
        out_specs=[pl.BlockSpec((steps, nb, A_WIDTH), lambda j: (j, 0, 0)),
                   const(nb, S5_STATES), const(nb, S5_STATES)],
        out_shape=[jax.ShapeDtypeStruct((seq, nb, A_WIDTH), F32), jax.ShapeDtypeStruct((nb, S5_STATES), F32),
                   jax.ShapeDtypeStruct((nb, S5_STATES), F32)],
        scratch_shapes=[pltpu.VMEM((rows, 2 * S5_STATES), F32), pltpu.VMEM((nb, S5_STATES), F32),
                        pltpu.VMEM((nb, S5_STATES), F32)],
        compiler_params=_cparams(("arbitrary",)), name="s5_glu",
    )(u_tb, h0_re, h0_im, a, win, cre, cim, d_skip.reshape(1, A_WIDTH), w_glu_bf16)


GDN_BA_COLS = 128
HI = lax.Precision.HIGHEST


def _gdn_pre_kernel(x_ref, conv0_ref, ba_ref, w_ref, alog_ref, dtb_ref, q_ref, k_ref, v_ref, bg_ref, xp_sc,
                    *, tm, tmo):
    j = pl.program_id(1)

    @pl.when(j == 0)
    def _():
        xp_sc[0:8, :] = conv0_ref[0]

    @pl.when(j > 0)
    def _():
        xp_sc[0:8, :] = xp_sc[tm:tm + 8, :]

    xp_sc[8:8 + tm, :] = x_ref[0]
    conv = w_ref[0:1, :] * xp_sc[pl.ds(5, tm), :]
    for jj in range(1, C_CONV):
        conv = conv + w_ref[jj:jj + 1, :] * xp_sc[pl.ds(5 + jj, tm), :]
    act = conv * jax.nn.sigmoid(conv)
    pad = tmo - tm

    def put(ref, val):
        if pad:
            val = jnp.concatenate([val, jnp.zeros((pad, val.shape[1]), F32)], axis=0)
        ref[0] = val

    qs, ks = [], []
    for h in range(C_HEADS):
        qh = act[:, h * C_DK:(h + 1) * C_DK]
        kh = act[:, C_QK + h * C_DK:C_QK + (h + 1) * C_DK]
        qs.append(qh * lax.rsqrt(jnp.sum(qh * qh, axis=-1, keepdims=True) + EPS) * (C_DK ** -0.5))
        ks.append(kh * lax.rsqrt(jnp.sum(kh * kh, axis=-1, keepdims=True) + EPS))
    put(q_ref, jnp.concatenate(qs, axis=-1))
    put(k_ref, jnp.concatenate(ks, axis=-1))
    put(v_ref, act[:, 2 * C_QK:])
    ba = ba_ref[0]
    lane = lax.broadcasted_iota(jnp.int32, ba.shape, 1)
    beta = jax.nn.sigmoid(ba)
    g = -jnp.exp(alog_ref[...]) * jax.nn.softplus(ba + dtb_ref[...])
    bg = jnp.where(lane < C_HEADS, beta, jnp.where(lane < 2 * C_HEADS, g, 0.0))
    put(bg_ref, bg)


def _gdn_pre(proj3, conv0, conv_w, a_log, dt_bias, tm, tmo):
    bsz, seq, _ = proj3.shape
    nblk = seq // tm
    conv0_p = jnp.pad(conv0, ((0, 0), (8 - (C_CONV - 1), 0), (0, 0)))
    vec = lambda x: jnp.zeros((1, GDN_BA_COLS), F32).at[0, C_HEADS:2 * C_HEADS].set(x)
    ba_blk = (C_CONV_CH + C_V) // GDN_BA_COLS
    out_len = nblk * tmo
    return pl.pallas_call(
        functools.partial(_gdn_pre_kernel, tm=tm, tmo=tmo),
        grid=(bsz, nblk),
        in_specs=[pl.BlockSpec((1, tm, C_CONV_CH), lambda b, j: (b, j, 0)),
                  pl.BlockSpec((1, 8, C_CONV_CH), lambda b, j: (b, 0, 0)),
                  pl.BlockSpec((1, tm, GDN_BA_COLS), lambda b, j: (b, j, ba_blk)),
                  pl.BlockSpec((C_CONV, C_CONV_CH), lambda b, j: (0, 0)),
                  pl.BlockSpec((1, GDN_BA_COLS), lambda b, j: (0, 0)),
                  pl.BlockSpec((1, GDN_BA_COLS), lambda b, j: (0, 0))],
        out_specs=[pl.BlockSpec((1, tmo, C_QK), lambda b, j: (b, j, 0)),
                   pl.BlockSpec((1, tmo, C_QK), lambda b, j: (b, j, 0)),
                   pl.BlockSpec((1, tmo, C_V), lambda b, j: (b, j, 0)),
                   pl.BlockSpec((1, tmo, GDN_BA_COLS), lambda b, j: (b, j, 0))],
        out_shape=[jax.ShapeDtypeStruct((bsz, out_len, C_QK), F32), jax.ShapeDtypeStruct((bsz, out_len, C_QK), F32),
                   jax.ShapeDtypeStruct((bsz, out_len, C_V), F32),
                   jax.ShapeDtypeStruct((bsz, out_len, GDN_BA_COLS), F32)],
        scratch_shapes=[pltpu.VMEM((tm + 16, C_CONV_CH), F32)],
        compiler_params=_cparams(("parallel", "arbitrary")), name="gdn_pre",
    )(proj3, conv0_p, proj3, conv_w, vec(a_log), vec(dt_bias))


def _dot_hi(a, b):
    return jnp.dot(a, b, preferred_element_type=F32, precision=HI)


def _dot_bf(a, b):
    return jnp.dot(a.astype(BF16), b.astype(BF16), preferred_element_type=F32)


def _dot_nt_hi(a, b):
    return lax.dot_general(a, b, (((1,), (1,)), ((), ())), preferred_element_type=F32, precision=HI)


def _gdn_chunk_kernel(q_ref, k_ref, v_ref, bg_ref, z_ref, s0_ref, ng_ref, y_ref, s_out, s_sc, *, chunk, nchunk, zlen):
    j = pl.program_id(1)

    @pl.when(j == 0)
    def _():
        s_sc[...] = s0_ref[0]

    ri = lax.broadcasted_iota(jnp.int32, (chunk, chunk), 0)
    ci = lax.broadcasted_iota(jnp.int32, (chunk, chunk), 1)
    causal = ci <= ri
    strict = ci < ri
    ltri = causal.astype(F32)
    eye = (ci == ri).astype(F32)
    e0 = (lax.broadcasted_iota(jnp.int32, (chunk, C_DK), 1) == 0).astype(F32)
    n_dbl = chunk.bit_length() - 2

    def do_chunk(c, carry):
        r = pl.ds(pl.multiple_of(c * chunk, chunk), chunk)
        bg = bg_ref[0, r, :]
        for h in range(C_HEADS):
            q = q_ref[0, r, h * C_DK:(h + 1) * C_DK]
            k = k_ref[0, r, h * C_DK:(h + 1) * C_DK]
            v = v_ref[0, r, h * C_DV:(h + 1) * C_DV]
            beta = bg[:, h:h + 1]
            g = bg[:, C_HEADS + h:C_HEADS + h + 1]
            gcol = _dot_hi(ltri, jnp.broadcast_to(g, (chunk, C_DK)))
            grow = _dot_nt_hi(e0, gcol)
            decay = jnp.where(causal, jnp.exp(jnp.where(causal, gcol[:, :chunk] - grow, 0.0)), 0.0)
            kb = k * beta
            vb = v * beta
            m = jnp.where(strict, _dot_nt_hi(kb, k) * decay, 0.0)
            tinv = eye - m
            mp = m
            for _ in range(n_dbl):
                mp = _dot_hi(mp, mp)
                tinv = tinv + _dot_hi(tinv, mp)
            tinv_b = tinv.astype(BF16)
            u = _dot_bf(tinv_b, vb)
            w = _dot_bf(tinv_b, kb * jnp.exp(gcol))
            s = s_sc[h]
            s_b = s.astype(BF16)
            v_new = u - _dot_bf(w, s_b)
            a_intra = jnp.where(causal, _dot_nt_hi(q, k), 0.0) * decay
            o = _dot_bf(q * jnp.exp(gcol), s_b) + _dot_bf(a_intra, v_new)
            g_last = gcol[chunk - 1:chunk, :]
            k2 = k * jnp.exp(g_last - gcol)
            s_sc[h] = s * jnp.exp(g_last) + lax.dot_general(k2.astype(BF16), v_new.astype(BF16),
                                                            (((0,), (0,)), ((), ())), preferred_element_type=F32)
            o = o * lax.rsqrt(jnp.mean(o * o, axis=-1, keepdims=True) + EPS) * ng_ref[...]
            z = z_ref[0, pl.ds(pl.multiple_of(c * chunk, chunk), zlen), h * C_DV:(h + 1) * C_DV]
            y_ref[0, pl.ds(pl.multiple_of(c * chunk, chunk), zlen), h * C_DV:(h + 1) * C_DV] = (
                o[:zlen] * (z * jax.nn.sigmoid(z)))
        return carry

    lax.fori_loop(0, nchunk, do_chunk, 0)

    @pl.when(j == pl.num_programs(1) - 1)
    def _():
        s_out[0] = s_sc[...]


def _gdn_chunks(q, k, v, bg, proj3, s0, norm_g, chunk, nchunk, zlen):
    bsz, out_len, _ = q.shape
    rb = chunk * nchunk
    nblk = out_len // rb
    zrows = zlen if zlen < chunk else rb
    z_blk = C_CONV_CH // C_V
    row = lambda w: pl.BlockSpec((1, rb, w), lambda b, j: (b, j, 0))
    return pl.pallas_call(
        functools.partial(_gdn_chunk_kernel, chunk=chunk, nchunk=nchunk, zlen=min(zlen, chunk)),
        grid=(bsz, nblk),
        in_specs=[row(C_QK), row(C_QK), row(C_V), row(GDN_BA_COLS),
                  pl.BlockSpec((1, zrows, C_V), lambda b, j: (b, j, z_blk)),
                  pl.BlockSpec((1, C_HEADS, C_DK, C_DV), lambda b, j: (b, 0, 0, 0)),
                  pl.BlockSpec((1, C_DV), lambda b, j: (0, 0))],
        out_specs=[pl.BlockSpec((1, zrows, C_V), lambda b, j: (b, j, 0)),
                   pl.BlockSpec((1, C_HEADS, C_DK, C_DV), lambda b, j: (b, 0, 0, 0))],
        out_shape=[jax.ShapeDtypeStruct((bsz, nblk * zrows, C_V), F32),
                   jax.ShapeDtypeStruct((bsz, C_HEADS, C_DK, C_DV), F32)],
        scratch_shapes=[pltpu.VMEM((C_HEADS, C_DK, C_DV), F32)],
        compiler_params=_cparams(("parallel", "arbitrary")), name="gdn_chunks",
    )(q, k, v, bg, proj3, s0, norm_g.reshape(1, C_DV))


def _tri_pairs(n):
    qi = [i for i in range(n) for _ in range(i + 1)]
    kj = [j for i in range(n) for j in range(i + 1)]
    return jnp.asarray(qi, jnp.int32), jnp.asarray(kj, jnp.int32)


def _softmax_update(s, v_bf16, m_ref, l_ref, acc_ref, idx):
    m_prev = m_ref[idx]
    m_new = jnp.maximum(m_prev, jnp.max(s, axis=-1, keepdims=True))
    corr = jnp.exp(m_prev - m_new)
    p = jnp.exp(s - m_new)
    l_ref[idx] = corr * l_ref[idx] + jnp.sum(p, axis=-1, keepdims=True)
    acc_ref[idx] = corr * acc_ref[idx] + jnp.dot(p.astype(BF16), v_bf16, preferred_element_type=F32)
    m_ref[idx] = m_new


def _diff_attn_kernel(qi_ref, kj_ref, lam_ref, q_ref, k_ref, v_ref, g_ref, o_ref, m_ref, l_ref, acc_ref,
                      *, blk, out_scale):
    p = pl.program_id(2)
    qi = qi_ref[p]
    kj = kj_ref[p]

    @pl.when(kj == 0)
    def _():
        m_ref[...] = jnp.full(m_ref.shape, NEG_BIG, F32)
        l_ref[...] = jnp.zeros(l_ref.shape, F32)
        acc_ref[...] = jnp.zeros(acc_ref.shape, F32)

    def step(masked):
        q = (q_ref[...] * (B_DK ** -0.5)).astype(BF16)
        k = k_ref[...].astype(BF16)
        v = v_ref[...].astype(BF16)
        for c in range(2):
            s = lax.dot_general(q[:, c * B_DK:(c + 1) * B_DK], k[:, c * B_DK:(c + 1) * B_DK],
                                (((1,), (1,)), ((), ())), preferred_element_type=F32)
            if masked:
                row = lax.broadcasted_iota(jnp.int32, s.shape, 0)
                col = lax.broadcasted_iota(jnp.int32, s.shape, 1)
                s = jnp.where(col <= row, s, NEG_BIG)
            _softmax_update(s, v, m_ref, l_ref, acc_ref, c)

    @pl.when(kj < qi)
    def _():
        step(False)

    @pl.when(kj == qi)
    def _():
        step(True)
        lam = lam_ref[0, 0]
        o0 = acc_ref[0] / l_ref[0]
        o1 = acc_ref[1] / l_ref[1]
        y = o0 - lam * o1
        y = y * lax.rsqrt(jnp.mean(y * y, axis=-1, keepdims=True) + EPS)
        o_ref[...] = y * g_ref[...] * out_scale


def _diff_attn_prompt(q, k, v, lam, subln, bsz, seq, lam_init, blk=512):
    nq = seq // blk
    qi, kj = _tri_pairs(nq)
    kern = functools.partial(_diff_attn_kernel, blk=blk, out_scale=1.0 - lam_init)
    grid_spec = pltpu.PrefetchScalarGridSpec(
        num_scalar_prefetch=2,
        grid=(bsz, B_HEADS, qi.shape[0]),
        in_specs=[
            pl.BlockSpec(memory_space=pltpu.SMEM),
            pl.BlockSpec((blk, B_DV), lambda b, h, p, qi, kj: (b * nq + qi[p], h)),
            pl.BlockSpec((blk, B_DV), lambda b, h, p, qi, kj: (b * nq + kj[p], h)),
            pl.BlockSpec((blk, B_DV), lambda b, h, p, qi, kj: (b * nq + kj[p], h)),
            pl.BlockSpec((1, B_DV), lambda b, h, p, qi, kj: (0, 0)),
        ],
        out_specs=pl.BlockSpec((blk, B_DV), lambda b, h, p, qi, kj: (b * nq + qi[p], h)),
        scratch_shapes=[pltpu.VMEM((2, blk, 1), F32), pltpu.VMEM((2, blk, 1), F32),
                        pltpu.VMEM((2, blk, B_DV), F32)],
    )
    return pl.pallas_call(
        kern, grid_spec=grid_spec, out_shape=jax.ShapeDtypeStruct(q.shape, F32),
        compiler_params=_cparams(("parallel", "parallel", "arbitrary")), name="diff_attn_prompt",
    )(qi, kj, lam.reshape(1, 1), q, k, v, subln.reshape(1, B_DV))


def _mla_attn_kernel(qi_ref, kj_ref, ql_ref, qr_ref, c_ref, kr_ref, wuv_ref, o_ref, m_ref, l_ref, acc_ref, *, blk):
    p = pl.program_id(1)
    qi = qi_ref[p]
    kj = kj_ref[p]
    scale = (D_NOPE + D_ROPE) ** -0.5

    @pl.when(kj == 0)
    def _():
        m_ref[...] = jnp.full(m_ref.shape, NEG_BIG, F32)
        l_ref[...] = jnp.zeros(l_ref.shape, F32)
        acc_ref[...] = jnp.zeros(acc_ref.shape, F32)

    def step(masked):
        c = c_ref[...].astype(BF16)
        kr = kr_ref[...].astype(BF16)
        for h in range(D_HEADS):
            ql = (ql_ref[:, h * D_KV_LORA:(h + 1) * D_KV_LORA] * scale).astype(BF16)
            qr = (qr_ref[:, h * D_ROPE:(h + 1) * D_ROPE] * scale).astype(BF16)
            s = (lax.dot_general(ql, c, (((1,), (1,)), ((), ())), preferred_element_type=F32)
                 + lax.dot_general(qr, kr, (((1,), (1,)), ((), ())), preferred_element_type=F32))
            if masked:
                row = lax.broadcasted_iota(jnp.int32, s.shape, 0)
                col = lax.broadcasted_iota(jnp.int32, s.shape, 1)
                s = jnp.where(col <= row, s, NEG_BIG)
            _softmax_update(s, c, m_ref, l_ref, acc_ref, h)

    @pl.when(kj < qi)
    def _():
        step(False)

    @pl.when(kj == qi)
    def _():
        step(True)
        for h in range(D_HEADS):
            o = (acc_ref[h] / l_ref[h]).astype(BF16)
            o_ref[:, h * D_DV:(h + 1) * D_DV] = jnp.dot(o, wuv_ref[h], preferred_element_type=F32)


def _mla_attn_prompt(q_lat, q_rope, c, kr, w_uv_bf16, bsz, seq, blk=512):
    nq = seq // blk
    qi, kj = _tri_pairs(nq)
    grid_spec = pltpu.PrefetchScalarGridSpec(
        num_scalar_prefetch=2,
        grid=(bsz, qi.shape[0]),
        in_specs=[
            pl.BlockSpec((blk, D_HEADS * D_KV_LORA), lambda b, p, qi, kj: (b * nq + qi[p], 0)),
            pl.BlockSpec((blk, D_HEADS * D_ROPE), lambda b, p, qi, kj: (b * nq + qi[p], 0)),
            pl.BlockSpec((blk, D_KV_LORA), lambda b, p, qi, kj: (b * nq + kj[p], 0)),
            pl.BlockSpec((blk, D_ROPE), lambda b, p, qi, kj: (b * nq + kj[p], 0)),
            pl.BlockSpec((D_HEADS, D_KV_LORA, D_DV), lambda b, p, qi, kj: (0, 0, 0)),
        ],
        out_specs=pl.BlockSpec((blk, D_HEADS * D_DV), lambda b, p, qi, kj: (b * nq + qi[p], 0)),
        scratch_shapes=[pltpu.VMEM((D_HEADS, blk, 1), F32), pltpu.VMEM((D_HEADS, blk, 1), F32),
                        pltpu.VMEM((D_HEADS, blk, D_KV_LORA), F32)],
    )
    return pl.pallas_call(
        functools.partial(_mla_attn_kernel, blk=blk), grid_spec=grid_spec,
        out_shape=jax.ShapeDtypeStruct((q_lat.shape[0], D_HEADS * D_DV), F32),
        compiler_params=_cparams(("parallel", "arbitrary")), name="mla_attn_prompt",
    )(qi, kj, q_lat, q_rope, c, kr, w_uv_bf16)


PAGES_PER_STEP = 8
NEW_PAD = 8


def _page_spec(width, i, n_pages, base):
    return pl.BlockSpec((1, PAGE_SIZE, width),
                        lambda b, p, pt: (pt[b * n_pages + p * PAGES_PER_STEP + i] + base, 0, 0))


def _flash_update(s, v_bf16, m_ref, l_ref, acc_ref):
    m_prev = m_ref[...]
    m_new = jnp.maximum(m_prev, jnp.max(s, axis=-1, keepdims=True))
    corr = jnp.exp(m_prev - m_new)
    p = jnp.exp(s - m_new)
    l_ref[...] = corr * l_ref[...] + jnp.sum(p, axis=-1, keepdims=True)
    acc_ref[...] = corr * acc_ref[...] + jnp.dot(p.astype(BF16), v_bf16, preferred_element_type=F32)
    m_ref[...] = m_new


def _self_mask(s, t_new):
    row_t = lax.broadcasted_iota(jnp.int32, s.shape, 0) % t_new
    col = lax.broadcasted_iota(jnp.int32, s.shape, 1)
    return jnp.where(col <= row_t, s, NEG_BIG)


def _diff_paged_kernel(pt_ref, lam_ref, q_ref, kn_ref, vn_ref, g_ref, *refs, t_new, out_scale):
    k_refs = refs[:PAGES_PER_STEP]
    v_refs = refs[PAGES_PER_STEP:2 * PAGES_PER_STEP]
    o_ref, m_ref, l_ref, acc_ref = refs[2 * PAGES_PER_STEP:]
    p = pl.program_id(1)

    @pl.when(p == 0)
    def _():
        m_ref[...] = jnp.full(m_ref.shape, NEG_BIG, F32)
        l_ref[...] = jnp.zeros(l_ref.shape, F32)
        acc_ref[...] = jnp.zeros(acc_ref.shape, F32)

    q = (q_ref[0] * (B_DK ** -0.5)).astype(BF16)
    for i in range(PAGES_PER_STEP):
        k = k_refs[i][0].astype(BF16)
        s = lax.dot_general(q, k, (((1,), (1,)), ((), ())), preferred_element_type=F32)
        _flash_update(s, v_refs[i][0].astype(BF16), m_ref, l_ref, acc_ref)

    @pl.when(p == pl.num_programs(1) - 1)
    def _():
        s = lax.dot_general(q, kn_ref[0].astype(BF16), (((1,), (1,)), ((), ())), preferred_element_type=F32)
        _flash_update(_self_mask(s, t_new), vn_ref[0].astype(BF16), m_ref, l_ref, acc_ref)
        o = acc_ref[...] / l_ref[...]
        half = B_HEADS * t_new
        y = o[:half] - lam_ref[0, 0] * o[half:]
        row_h = lax.broadcasted_iota(jnp.int32, y.shape, 0) // t_new
        col_h = lax.broadcasted_iota(jnp.int32, y.shape, 1) // B_DV
        y = jnp.where(row_h == col_h, y, 0.0)
        sel_r = lax.broadcasted_iota(jnp.int32, (NEW_PAD, half), 0)
        sel_c = lax.broadcasted_iota(jnp.int32, (NEW_PAD, half), 1)
        sel = (sel_c % t_new == sel_r).astype(F32)
        y = _dot_hi(sel, y)
        outs = []
        for h in range(B_HEADS):
            yh = y[:, h * B_DV:(h + 1) * B_DV]
            outs.append(yh * lax.rsqrt(jnp.mean(yh * yh, axis=-1, keepdims=True) + EPS) * g_ref[...] * out_scale)
        o_ref[0] = jnp.concatenate(outs, axis=-1)


def _diff_attn_paged(q, k, v, cache_k, cache_v, layer, page_table, lam, subln, lam_init):
    bd, t_new = q.shape[:2]
    n_pool = cache_k.shape[1]
    n_pages = page_table.shape[1]
    ck = cache_k.reshape(-1, PAGE_SIZE, B_HEADS * 2 * B_DK)
    cv = cache_v.reshape(-1, PAGE_SIZE, B_HEADS * B_DV)
    eye_h = jnp.eye(B_HEADS, dtype=F32)
    eye_c = jnp.eye(2, dtype=F32)
    qbd = jnp.einsum('bthcd,hg,ce->bchtged', q, eye_h, eye_c).reshape(bd, 2 * B_HEADS * t_new, B_HEADS * 2 * B_DK)
    padn = lambda x: jnp.pad(x.reshape(bd, t_new, -1), ((0, 0), (0, NEW_PAD - t_new), (0, 0)))
    rows = 2 * B_HEADS * t_new
    width = B_HEADS * B_DV
    per_b = lambda r, w: pl.BlockSpec((1, r, w), lambda b, p, pt: (b, 0, 0))
    grid_spec = pltpu.PrefetchScalarGridSpec(
        num_scalar_prefetch=1,
        grid=(bd, n_pages // PAGES_PER_STEP),
        in_specs=[pl.BlockSpec(memory_space=pltpu.SMEM), per_b(rows, width), per_b(NEW_PAD, width),
                  per_b(NEW_PAD, width), pl.BlockSpec((1, B_DV), lambda b, p, pt: (0, 0))]
        + [_page_spec(width, i, n_pages, layer * n_pool) for i in range(PAGES_PER_STEP)] * 2,
        out_specs=per_b(NEW_PAD, width),
        scratch_shapes=[pltpu.VMEM((rows, 1), F32), pltpu.VMEM((rows, 1), F32), pltpu.VMEM((rows, width), F32)],
    )
    y = pl.pallas_call(
        functools.partial(_diff_paged_kernel, t_new=t_new, out_scale=1.0 - lam_init), grid_spec=grid_spec,
        out_shape=jax.ShapeDtypeStruct((bd, NEW_PAD, width), F32),
        compiler_params=_cparams(("parallel", "arbitrary")), name="diff_attn_paged",
    )(page_table.reshape(-1), lam.reshape(1, 1), qbd, padn(k), padn(v), subln.reshape(1, B_DV),
      *([ck] * PAGES_PER_STEP), *([cv] * PAGES_PER_STEP))
    return y[:, :t_new]


def _mla_paged_kernel(pt_ref, ql_ref, qr_ref, cn_ref, rn_ref, *refs, t_new):
    c_refs = refs[:PAGES_PER_STEP]
    r_refs = refs[PAGES_PER_STEP:2 * PAGES_PER_STEP]
    o_ref, m_ref, l_ref, acc_ref = refs[2 * PAGES_PER_STEP:]
    p = pl.program_id(1)
    scale = (D_NOPE + D_ROPE) ** -0.5

    @pl.when(p == 0)
    def _():
        m_ref[...] = jnp.full(m_ref.shape, NEG_BIG, F32)
        l_ref[...] = jnp.zeros(l_ref.shape, F32)
        acc_ref[...] = jnp.zeros(acc_ref.shape, F32)

    ql = (ql_ref[0] * scale).astype(BF16)
    qr = (qr_ref[0] * scale).astype(BF16)
    nt = (((1,), (1,)), ((), ()))

    def scores(c_bf16, r_bf16):
        return (lax.dot_general(ql, c_bf16, nt, preferred_element_type=F32)
                + lax.dot_general(qr, r_bf16, nt, preferred_element_type=F32))

    for i in range(PAGES_PER_STEP):
        c = c_refs[i][0].astype(BF16)
        _flash_update(scores(c, r_refs[i][0].astype(BF16)), c, m_ref, l_ref, acc_ref)

    @pl.when(p == pl.num_programs(1) - 1)
    def _():
        c = cn_ref[0].astype(BF16)
        _flash_update(_self_mask(scores(c, rn_ref[0].astype(BF16)), t_new), c, m_ref, l_ref, acc_ref)
        o_ref[0] = acc_ref[...] / l_ref[...]


def _mla_attn_paged(q_lat, q_rope, c, kr, cache_lat, cache_rope, layer, page_table):
    bd, t_new = q_lat.shape[:2]
    n_pool = cache_lat.shape[1]
    n_pages = page_table.shape[1]
    cl = cache_lat.reshape(-1, PAGE_SIZE, D_KV_LORA)
    cr = cache_rope.reshape(-1, PAGE_SIZE, D_ROPE)
    rows = D_HEADS * t_new
    ql = q_lat.transpose(0, 2, 1, 3).reshape(bd, rows, D_KV_LORA)
    qr = q_rope.transpose(0, 2, 1, 3).reshape(bd, rows, D_ROPE)
    padn = lambda x: jnp.pad(x, ((0, 0), (0, NEW_PAD - t_new), (0, 0)))
    per_b = lambda r, w: pl.BlockSpec((1, r, w), lambda b, p, pt: (b, 0, 0))
    grid_spec = pltpu.PrefetchScalarGridSpec(
        num_scalar_prefetch=1,
        grid=(bd, n_pages // PAGES_PER_STEP),
        in_specs=[per_b(rows, D_KV_LORA), per_b(rows, D_ROPE), per_b(NEW_PAD, D_KV_LORA), per_b(NEW_PAD, D_ROPE)]
        + [_page_spec(D_KV_LORA, i, n_pages, layer * n_pool) for i in range(PAGES_PER_STEP)]
        + [_page_spec(D_ROPE, i, n_pages, layer * n_pool) for i in range(PAGES_PER_STEP)],
        out_specs=per_b(rows, D_KV_LORA),
        scratch_shapes=[pltpu.VMEM((rows, 1), F32), pltpu.VMEM((rows, 1), F32), pltpu.VMEM((rows, D_KV_LORA), F32)],
    )
    o = pl.pallas_call(
        functools.partial(_mla_paged_kernel, t_new=t_new), grid_spec=grid_spec,
        out_shape=jax.ShapeDtypeStruct((bd, rows, D_KV_LORA), F32),
        compiler_params=_cparams(("parallel", "arbitrary")), name="mla_attn_paged",
    )(page_table.reshape(-1), ql, qr, padn(c), padn(kr), *([cl] * PAGES_PER_STEP), *([cr] * PAGES_PER_STEP))
    return o.reshape(bd, D_HEADS, t_new, D_KV_LORA).transpose(0, 2, 1, 3)


ROUTER_PAD = 128


def _out_ln_router_kernel(xa_ref, xb_ref, wa_ref, wb_ref, h_ref, g_ref, b_ref, rw_ref, rb_ref, o_ref, lg_ref):
    mix = (jnp.dot(xa_ref[...].astype(BF16), wa_ref[...], preferred_element_type=F32)
           + jnp.dot(xb_ref[...].astype(BF16), wb_ref[...], preferred_element_type=F32))
    x = DN_ALPHA * h_ref[...] + mix
    mu = jnp.mean(x, axis=-1, keepdims=True)
    xc = x - mu
    var = jnp.mean(xc * xc, axis=-1, keepdims=True)
    y = xc * lax.rsqrt(var + EPS) * g_ref[...] + b_ref[...]
    o_ref[...] = y
    lg_ref[...] = _dot_hi(y, rw_ref[...]) + rb_ref[...]


def _out_ln_router(xa, xb, w_out, h, ln_g, ln_b, router_w, router_b, tm=256):
    n, d = h.shape
    ka, kb = xa.shape[1], xb.shape[1]
    tm = min(tm, n)
    rw = jnp.pad(router_w, ((0, 0), (0, ROUTER_PAD - N_EXPERTS)))
    rb = jnp.pad(router_b, (0, ROUTER_PAD - N_EXPERTS)).reshape(1, ROUTER_PAD)
    row = lambda w: pl.BlockSpec((tm, w), lambda i: (i, 0))
    const = lambda r, w: pl.BlockSpec((r, w), lambda i: (0, 0))
    return pl.pallas_call(
        _out_ln_router_kernel, grid=(n // tm,),
        in_specs=[row(ka), row(kb), const(ka, d), const(kb, d), row(d), const(1, d), const(1, d),
                  const(d, ROUTER_PAD), const(1, ROUTER_PAD)],
        out_specs=[row(d), row(ROUTER_PAD)],
        out_shape=[jax.ShapeDtypeStruct((n, d), F32), jax.ShapeDtypeStruct((n, ROUTER_PAD), F32)],
        compiler_params=_cparams(("parallel",)), name="out_ln_router",
    )(xa, xb, w_out[:ka].astype(BF16), w_out[ka:].astype(BF16), h, ln_g.reshape(1, d), ln_b.reshape(1, d), rw, rb)


def _combine_ln_kernel(ya_ref, gt_ref, h_ref, g_ref, b_ref, o_ref):
    d = h_ref.shape[1]
    gates = gt_ref[...]
    f = gates[:, 0:1] * ya_ref[:, 0:d]
    for kk in range(1, TOP_K):
        f = f + gates[:, kk:kk + 1] * ya_ref[:, kk * d:(kk + 1) * d]
    x = DN_ALPHA * h_ref[...] + f
    mu = jnp.mean(x, axis=-1, keepdims=True)
    xc = x - mu
    var = jnp.mean(xc * xc, axis=-1, keepdims=True)
    o_ref[...] = xc * lax.rsqrt(var + EPS) * g_ref[...] + b_ref[...]


def _combine_ln(y_assign, gates, h, ln_g, ln_b, tm=256):
    n, d = h.shape
    tm = min(tm, n)
    row = lambda w: pl.BlockSpec((tm, w), lambda i: (i, 0))
    const = lambda r, w: pl.BlockSpec((r, w), lambda i: (0, 0))
    return pl.pallas_call(
        _combine_ln_kernel, grid=(n // tm,),
        in_specs=[row(TOP_K * d), row(TOP_K), row(d), const(1, d), const(1, d)],
        out_specs=row(d), out_shape=jax.ShapeDtypeStruct((n, d), F32),
        compiler_params=_cparams(("parallel",)), name="combine_ln",
    )(y_assign, gates, h, ln_g.reshape(1, d), ln_b.reshape(1, d))


def _moe_kernel(be_ref, nu_ref, x_ref, wgu_ref, bgu_ref, wd_ref, bd_ref, o_ref, wgu_bf, wd_bf):
    i = pl.program_id(0)
    e = be_ref[i]
    e_prev = be_ref[jnp.maximum(i - 1, 0)]

    @pl.when((i == 0) | (e != e_prev))
    def _():
        wgu_bf[...] = wgu_ref[0].astype(BF16)
        wd_bf[...] = wd_ref[0].astype(BF16)

    @pl.when(i < nu_ref[0])
    def _():
        h = jnp.dot(x_ref[...].astype(BF16), wgu_bf[...], preferred_element_type=F32) + bgu_ref[0]
        gate = jnp.minimum(h[:, :D_FF], SWIGLU_LIMIT)
        up = jnp.clip(h[:, D_FF:], -SWIGLU_LIMIT, SWIGLU_LIMIT)
        act = (up + 1.0) * gate * jax.nn.sigmoid(SWIGLU_ALPHA * gate)
        o_ref[...] = jnp.dot(act.astype(BF16), wd_bf[...], preferred_element_type=F32) + bd_ref[0]

    @pl.when(i >= nu_ref[0])
    def _():
        o_ref[...] = jnp.zeros(o_ref.shape, F32)


def _moe_experts(x_sorted, block_e, n_used, w_gu, b_gu, w_down, b_down, blk):
    p_rows, d = x_sorted.shape
    n_blocks = p_rows // blk
    grid_spec = pltpu.PrefetchScalarGridSpec(
        num_scalar_prefetch=2,
        grid=(n_blocks,),
        in_specs=[
            pl.BlockSpec((blk, d), lambda i, be, nu: (i, 0)),
            pl.BlockSpec((1, d, 2 * D_FF), lambda i, be, nu: (be[i], 0, 0)),
            pl.BlockSpec((1, 1, 2 * D_FF), lambda i, be, nu: (be[i], 0, 0)),
            pl.BlockSpec((1, D_FF, d), lambda i, be, nu: (be[i], 0, 0)),
            pl.BlockSpec((1, 1, d), lambda i, be, nu: (be[i], 0, 0)),
        ],
        out_specs=pl.BlockSpec((blk, d), lambda i, be, nu: (i, 0)),
        scratch_shapes=[pltpu.VMEM((d, 2 * D_FF), BF16), pltpu.VMEM((D_FF, d), BF16)],
    )
    return pl.pallas_call(
        _moe_kernel, grid_spec=grid_spec, out_shape=jax.ShapeDtypeStruct((p_rows, d), F32),
        compiler_params=_cparams(("arbitrary",)), name="moe_experts",
    )(block_e, n_used, x_sorted, w_gu, b_gu.reshape(N_EXPERTS, 1, 2 * D_FF), w_down,
      b_down.reshape(N_EXPERTS, 1, d))


def _moe(xt, logits, w_gu, b_gu, w_down, b_down, ln_g, ln_b):
    n, d = xt.shape
    blk = 256 if n >= 8192 else 128
    top_val, top_idx = lax.top_k(logits, TOP_K)
    gates = jax.nn.softmax(top_val, axis=-1)
    nk = n * TOP_K
    flat_e = top_idx.reshape(-1).astype(jnp.int32)
    onehot = (flat_e[:, None] == jnp.arange(N_EXPERTS, dtype=jnp.int32)[None, :]).astype(jnp.int32)
    csum = jnp.cumsum(onehot, axis=0)
    counts = csum[-1]
    rank = jnp.sum((csum - onehot) * onehot, axis=1)
    padded = (counts + blk - 1) // blk * blk
    pend = jnp.cumsum(padded)
    pstart = pend - padded
    dest = pstart[flat_e] + rank
    n_blocks = -(-(nk + N_EXPERTS * (blk - 1)) // blk)
    p_rows = n_blocks * blk
    buf_tok = jnp.zeros((p_rows,), jnp.int32).at[dest].set(jnp.arange(nk, dtype=jnp.int32) // TOP_K,
                                                            unique_indices=True)
    block_e = jnp.minimum(jnp.searchsorted(pend, jnp.arange(n_blocks, dtype=jnp.int32) * blk, side='right'),
                          N_EXPERTS - 1).astype(jnp.int32)
    n_used = (pend[-1] // blk).astype(jnp.int32).reshape(1)
    x_sorted = xt[buf_tok]
    y_buf = _moe_experts(x_sorted, block_e, n_used, w_gu, b_gu, w_down, b_down, blk)
    y_assign = y_buf[dest].reshape(n, TOP_K * d)
    return _combine_ln(y_assign, gates, xt, ln_g, ln_b)


def _rms_norm(x, g):
    return x * lax.rsqrt(jnp.mean(x * x, axis=-1, keepdims=True) + EPS) * g


def _rope(x, pos, theta, n_rot):
    half = n_rot // 2
    inv = jnp.exp(-math.log(theta) * jnp.arange(half, dtype=F32) * (2.0 / n_rot))
    ang = pos[:, None] * inv[None, :]
    cos = jnp.cos(ang)[:, None, :]
    sin = jnp.sin(ang)[:, None, :]
    x1 = x[..., :half]
    x2 = x[..., half:n_rot]
    return jnp.concatenate([x1 * cos - x2 * sin, x2 * cos + x1 * sin, x[..., n_rot:]], axis=-1)


def _even_mixer(x, pos, h0_re, h0_im, sample_ctx, lam_init, w_in, lam_re, lam_im, log_dt, b_re, b_im,
                c_re, c_im, d_skip, w_glu, lq1, lk1, lq2, lk2, subln, w_out):
    bsz, L, _ = x.shape
    n = bsz * L
    proj = _mm(x.reshape(n, D_MODEL), w_in.astype(BF16))
    u_tb = proj[:, :A_WIDTH].reshape(bsz, L, A_WIDTH).transpose(1, 0, 2)
    q = proj[:, A_WIDTH:A_WIDTH + B_QK]
    k = proj[:, A_WIDTH + B_QK:A_WIDTH + 2 * B_QK]
    v = proj[:, A_WIDTH + 2 * B_QK:]
    s5w = _s5_weights(lam_re, lam_im, log_dt, b_re, b_im, c_re, c_im)
    y_tb, h_re, h_im = _s5_glu(u_tb, h0_re.reshape(bsz, S5_STATES), h0_im.reshape(bsz, S5_STATES), s5w,
                               d_skip, w_glu.astype(BF16), steps=min(L, 16))
    y_a = y_tb.transpose(1, 0, 2).reshape(n, A_WIDTH)
    h_re = h_re.reshape(bsz, A_GROUPS, A_STATE)
    h_im = h_im.reshape(bsz, A_GROUPS, A_STATE)
    q = _rope(q.reshape(bsz, L, 2 * B_HEADS, B_DK), pos, ROPE_THETA, B_ROT)
    k = _rope(k.reshape(bsz, L, 2 * B_HEADS, B_DK), pos, ROPE_THETA, B_ROT)
    lam = jnp.exp(jnp.sum(lq1 * lk1)) - jnp.exp(jnp.sum(lq2 * lk2)) + lam_init
    if sample_ctx is None:
        y_b = _diff_attn_prompt(q.reshape(n, B_QK), k.reshape(n, B_QK), v, lam, subln, bsz, L, lam_init)
    else:
        cache_k, cache_v, layer, page_table = sample_ctx
        y_b = _diff_attn_paged(q.reshape(bsz, L, B_HEADS, 2, B_DK), k.reshape(bsz, L, B_HEADS, 2, B_DK),
                               v.reshape(bsz, L, B_HEADS, B_DV), cache_k, cache_v, layer, page_table, lam, subln,
                               lam_init).reshape(n, B_HEADS * B_DV)
    return ((y_a, y_b, w_out), h_re, h_im, k.reshape(bsz, L, B_HEADS, 2 * B_DK),
            v.reshape(bsz, L, B_HEADS, B_DV))


def _odd_mixer(x, pos, conv0, s0, sample_ctx, w_in, conv_w, a_log, dt_bias, gdn_g,
               q_norm, w_uq, kv_norm, w_uk, w_uv, w_out):
    bsz, L, _ = x.shape
    n = bsz * L
    zcols = lambda w: jnp.zeros((D_MODEL, w), F32)
    ba_end = C_CONV_CH + C_V + 2 * C_HEADS
    w_in_p = jnp.concatenate([w_in[:, :ba_end], zcols(GDN_BA_COLS - 2 * C_HEADS), w_in[:, ba_end:],
                              zcols(128 - D_ROPE)], axis=1).astype(BF16)
    o_cq = C_CONV_CH + C_V + GDN_BA_COLS
    o_ckv = o_cq + D_Q_LORA
    o_kr = o_ckv + D_KV_LORA
    proj = _mm(x.reshape(n, D_MODEL), w_in_p)
    proj3 = proj.reshape(bsz, L, proj.shape[1])
    if L >= C_CHUNK:
        tm, tmo, chunk, nchunk, zlen = 512, 512, C_CHUNK, 512 // C_CHUNK, C_CHUNK
    else:
        tm, tmo, chunk, nchunk, zlen = L, 8, 8, 1, L
    gq, gk, gv, gbg = _gdn_pre(proj3, conv0, conv_w, a_log, dt_bias, tm, tmo)
    y_c, S = _gdn_chunks(gq, gk, gv, gbg, proj3, s0, gdn_g, chunk, nchunk, zlen)
    conv_new = proj3[:, L - (C_CONV - 1):, :C_CONV_CH]
    c_q = proj3[..., o_cq:o_ckv]
    c_kv = proj3[..., o_ckv:o_kr]
    k_r = proj3[..., o_kr:o_kr + D_ROPE]
    qf = _mm(_rms_norm(c_q, q_norm).reshape(n, D_Q_LORA), w_uq.astype(BF16))
    qf = qf.reshape(bsz, L, D_HEADS, D_NOPE + D_ROPE)
    q_rope = _rope(qf[..., D_NOPE:], pos, MLA_ROPE_THETA, D_ROPE)
    q_nope = qf[..., :D_NOPE].reshape(n, D_HEADS * D_NOPE)
    w_bd = jnp.zeros((D_HEADS * D_NOPE, D_HEADS * D_KV_LORA), F32)
    for h in range(D_HEADS):
        w_bd = w_bd.at[h * D_NOPE:(h + 1) * D_NOPE, h * D_KV_LORA:(h + 1) * D_KV_LORA].set(w_uk[h])
    q_lat = _mm(q_nope, w_bd.astype(BF16))
    c = _rms_norm(c_kv, kv_norm)
    kr = _rope(k_r[:, :, None, :], pos, MLA_ROPE_THETA, D_ROPE)[:, :, 0, :]
    if sample_ctx is None:
        y_d = _mla_attn_prompt(q_lat, q_rope.reshape(n, D_HEADS * D_ROPE), c.reshape(n, D_KV_LORA),
                               kr.reshape(n, D_ROPE), w_uv.astype(BF16), bsz, L)
    else:
        cache_lat, cache_rope, layer, page_table = sample_ctx
        o_lat = _mla_attn_paged(q_lat.reshape(bsz, L, D_HEADS, D_KV_LORA), q_rope, c, kr, cache_lat, cache_rope,
                                layer, page_table)
        w_uv_bd = jnp.zeros((D_HEADS * D_KV_LORA, D_HEADS * D_DV), F32)
        for h in range(D_HEADS):
            w_uv_bd = w_uv_bd.at[h * D_KV_LORA:(h + 1) * D_KV_LORA, h * D_DV:(h + 1) * D_DV].set(w_uv[h])
        y_d = _mm(o_lat.reshape(n, D_HEADS * D_KV_LORA), w_uv_bd.astype(BF16))
    return (y_c.reshape(n, C_V), y_d, w_out), S, conv_new, c, kr


def _post_block(h, mix, li, router_w, router_b, moe_w_gu, moe_b_gu, moe_w_down, moe_b_down, ln_g, ln_b):
    bsz, L, d = h.shape
    xa, xb, w_out = mix
    h1, logits = _out_ln_router(xa, xb, w_out, h.reshape(-1, d), ln_g[li, 0], ln_b[li, 0], router_w[li],
                                router_b[li])
    h2 = _moe(h1, logits[:, :N_EXPERTS], moe_w_gu[li], moe_b_gu[li], moe_w_down[li], moe_b_down[li],
              ln_g[li, 1], ln_b[li, 1])
    return h2.reshape(bsz, L, d)


def kernel(x_prompt, x_sample, state_a_re, state_a_im, cache_b_k, cache_b_v, state_c, state_c_conv,
           cache_d_latent, cache_d_rope, page_table, w_in_even, s5_lam_re, s5_lam_im, s5_log_dt,
           s5_b_re, s5_b_im, s5_c_re, s5_c_im, s5_d, s5_w_glu, diff_lq1, diff_lk1, diff_lq2, diff_lk2,
           diff_subln, w_out_even, w_in_odd, gdn_conv_w, gdn_a_log, gdn_dt_bias, gdn_norm, mla_q_norm,
           mla_w_uq, mla_kv_norm, mla_w_uk, mla_w_uv, w_out_odd, router_w, router_b, moe_w_gu, moe_b_gu,
           moe_w_down, moe_b_down, ln_g, ln_b):
    bp, lp, _ = x_prompt.shape
    bs, ls, _ = x_sample.shape
    pos_p = jnp.arange(lp, dtype=F32)
    pos_s = PAST_LEN + jnp.arange(ls, dtype=F32)
    hp, hs = x_prompt, x_sample
    outs_p = {k: [] for k in ('a_re', 'a_im', 'b_k', 'b_v', 'c', 'c_conv', 'd_lat', 'd_rope')}
    outs_s = {k: [] for k in outs_p}
    for li in range(DEPTH):
        if li % 2 == 0:
            e = li // 2
            lam_init = 0.8 - 0.6 * math.exp(-0.3 * li)
            ew = (w_in_even[e], s5_lam_re[e], s5_lam_im[e], s5_log_dt[e], s5_b_re[e], s5_b_im[e],
                  s5_c_re[e], s5_c_im[e], s5_d[e], s5_w_glu[e], diff_lq1[e], diff_lk1[e], diff_lq2[e],
                  diff_lk2[e], diff_subln[e], w_out_even[e])
            z0 = jnp.zeros((bp, A_GROUPS, A_STATE), F32)
            mp, hr, hi, kk, vv = _even_mixer(hp, pos_p, z0, z0, None, lam_init, *ew)
            for key, val in zip(('a_re', 'a_im', 'b_k', 'b_v'), (hr, hi, kk, vv)):
                outs_p[key].append(val)
            ctx = (cache_b_k, cache_b_v, e, page_table)
            ms, hr, hi, kk, vv = _even_mixer(hs, pos_s, state_a_re[e], state_a_im[e], ctx, lam_init, *ew)
            for key, val in zip(('a_re', 'a_im', 'b_k', 'b_v'), (hr, hi, kk, vv)):
                outs_s[key].append(val)
        else:
            o = li // 2
            ow = (w_in_odd[o], gdn_conv_w[o], gdn_a_log[o], gdn_dt_bias[o], gdn_norm[o], mla_q_norm[o],
                  mla_w_uq[o], mla_kv_norm[o], mla_w_uk[o], mla_w_uv[o], w_out_odd[o])
            conv0 = jnp.zeros((bp, C_CONV - 1, C_CONV_CH), F32)
            s0 = jnp.zeros((bp, C_HEADS, C_DK, C_DV), F32)
            mp, S, cb, cl, kr = _odd_mixer(hp, pos_p, conv0, s0, None, *ow)
            for key, val in zip(('c', 'c_conv', 'd_lat', 'd_rope'), (S, cb, cl, kr)):
                outs_p[key].append(val)
            ctx = (cache_d_latent, cache_d_rope, o, page_table)
            ms, S, cb, cl, kr = _odd_mixer(hs, pos_s, state_c_conv[o], state_c[o], ctx, *ow)
            for key, val in zip(('c', 'c_conv', 'd_lat', 'd_rope'), (S, cb, cl, kr)):
                outs_s[key].append(val)
        post = (li, router_w, router_b, moe_w_gu, moe_b_gu, moe_w_down, moe_b_down, ln_g, ln_b)
        hp = _post_block(hp, mp, *post)
        hs = _post_block(hs, ms, *post)
    keys = ('a_re', 'a_im', 'b_k', 'b_v', 'c', 'c_conv', 'd_lat', 'd_rope')
    return ((hp, hs) + tuple(jnp.stack(outs_p[k]) for k in keys) + tuple(jnp.stack(outs_s[k]) for k in keys))
```

```python
import functools
import math

import jax
import jax.numpy as jnp
from jax import lax
from jax.experimental import pallas as pl
from jax.experimental.pallas import tpu as pltpu

D_MODEL = 1024
DEPTH = 2
PAST_LEN = 16384
PAGE_SIZE = 128
A_WIDTH = D_MODEL // 2
A_GROUP = 16
A_GROUPS = A_WIDTH // A_GROUP
A_STATE = 64
B_HEADS = 4
B_DK = 64
B_DV = 2 * B_DK
B_ROT = B_DK // 4
B_QK = B_HEADS * 2 * B_DK
C_HEADS = 4
C_DK = 128
C_DV = 128
C_CONV = 4
C_CHUNK = 64
C_QK = C_HEADS * C_DK
C_V = C_HEADS * C_DV
C_CONV_CH = 2 * C_QK + C_V
C_IN = C_CONV_CH + C_V + 2 * C_HEADS
D_HEADS = 4
D_NOPE = 128
D_ROPE = 32
D_DV = 128
D_Q_LORA = 384
D_KV_LORA = 256
ROPE_THETA = 500000.0
MLA_ROPE_THETA = 10000.0
N_EXPERTS = 32
TOP_K = 4
D_FF = 1024
SWIGLU_LIMIT = 7.0
SWIGLU_ALPHA = 1.702
DN_ALPHA = (2 * DEPTH) ** 0.25
EPS = 1e-6
NEG_BIG = -1e30

V7X_VMEM_BYTES = 64 * 1024 * 1024
VMEM_LIMIT = V7X_VMEM_BYTES * 7 // 8
BF16 = jnp.bfloat16
F32 = jnp.float32


def _cparams(sem):
    return pltpu.CompilerParams(dimension_semantics=sem, vmem_limit_bytes=VMEM_LIMIT)


def _mm_kernel(x_ref, w_ref, o_ref):
    o_ref[...] = jnp.dot(x_ref[...].astype(BF16), w_ref[...], preferred_element_type=F32)


def _mm(x, w_bf16, tm=256):
    m, k = x.shape
    n = w_bf16.shape[1]
    tm = min(tm, m)
    return pl.pallas_call(
        _mm_kernel,
        grid=(m // tm,),
        in_specs=[pl.BlockSpec((tm, k), lambda i: (i, 0)), pl.BlockSpec((k, n), lambda i: (0, 0))],
        out_specs=pl.BlockSpec((tm, n), lambda i: (i, 0)),
        out_shape=jax.ShapeDtypeStruct((m, n), F32),
        compiler_params=_cparams(("parallel",)),
        name="mm",
    )(x, w_bf16)


S5_HALF_IN = A_WIDTH // 2
S5_HALF_ST = A_GROUPS * A_STATE // 2
S5_STATES = A_GROUPS * A_STATE


def _s5_kernel(u_ref, h0r_ref, h0i_ref, a_ref, win_ref, cre_ref, cim_ref, d_ref, wglu_ref, *rest, nb, steps,
               batch_major):
    if batch_major:
        perm_ref, perm_t_ref = rest[:2]
        rest = rest[2:]
    y_ref, hr_out, hi_out, x_sc, hr_sc, hi_sc = rest
    j = pl.program_id(0)

    @pl.when(j == 0)
    def _():
        hr_sc[...] = h0r_ref[...]
        hi_sc[...] = h0i_ref[...]

    rows = steps * nb
    u = u_ref[...].reshape(rows, A_WIDTH)
    ub = u.astype(BF16)
    if batch_major:
        ub = jnp.dot(perm_ref[...], ub, preferred_element_type=F32).astype(BF16)
    for k in range(2):
        xk = jnp.dot(ub[:, k * S5_HALF_IN:(k + 1) * S5_HALF_IN], win_ref[k], preferred_element_type=F32)
        x_sc[:, k * S5_HALF_ST:(k + 1) * S5_HALF_ST] = xk[:, :S5_HALF_ST]
        x_sc[:, S5_STATES + k * S5_HALF_ST:S5_STATES + (k + 1) * S5_HALF_ST] = xk[:, S5_HALF_ST:]
    ar = a_ref[0:1, :]
    ai = a_ref[1:2, :]

    def body(t, carry):
        r = pl.ds(pl.multiple_of(t * nb, nb), nb)
        hr = hr_sc[...]
        hi = hi_sc[...]
        nr = ar * hr - ai * hi + x_sc[r, 0:S5_STATES]
        ni = ar * hi + ai * hr + x_sc[r, S5_STATES:2 * S5_STATES]
        x_sc[r, 0:S5_STATES] = nr
        x_sc[r, S5_STATES:2 * S5_STATES] = ni
        hr_sc[...] = nr
        hi_sc[...] = ni
        return carry

    lax.fori_loop(0, steps, body, 0)
    ys = []
    for k in range(2):
        hk_r = x_sc[:, k * S5_HALF_ST:(k + 1) * S5_HALF_ST].astype(BF16)
        hk_i = x_sc[:, S5_STATES + k * S5_HALF_ST:S5_STATES + (k + 1) * S5_HALF_ST].astype(BF16)
        yk = (jnp.dot(hk_r, cre_ref[k], preferred_element_type=F32)
              - jnp.dot(hk_i, cim_ref[k], preferred_element_type=F32))
        ys.append(yk)
    y = jnp.concatenate(ys, axis=-1)
    if batch_major:
        y = _dot_sel(perm_t_ref[...], y)
    y = y + d_ref[...] * u
    y = jax.nn.gelu(y)
    y = y * jax.nn.sigmoid(jnp.dot(y.astype(BF16), wglu_ref[...], preferred_element_type=F32))
    y_ref[...] = y.reshape(y_ref.shape)

    @pl.when(j == pl.num_programs(0) - 1)
    def _():
        hr_out[...] = hr_sc[...]
        hi_out[...] = hi_sc[...]


def _s5_weights(lam_re, lam_im, log_dt, b_re, b_im, c_re, c_im):
    dt = jnp.exp(log_dt)[:, None]
    mag = jnp.exp(lam_re * dt)
    ar = mag * jnp.cos(lam_im * dt)
    ai = mag * jnp.sin(lam_im * dt)
    den = lam_re * lam_re + lam_im * lam_im
    fr = ((ar - 1.0) * lam_re + ai * lam_im) / den
    fi = (ai * lam_re - (ar - 1.0) * lam_im) / den
    bbr = fr[..., None] * b_re - fi[..., None] * b_im
    bbi = fr[..., None] * b_im + fi[..., None] * b_re
    eye = jnp.eye(A_GROUPS, dtype=F32)

    def in_map(bb):
        return jnp.einsum('gh,gnp->gphn', eye, bb).reshape(A_WIDTH, S5_STATES)

    def out_map(c):
        return jnp.einsum('gh,gpn->gnhp', eye, c).reshape(S5_STATES, A_WIDTH)

    wr, wi = in_map(bbr), in_map(bbi)
    win = jnp.stack([jnp.concatenate([w[k * S5_HALF_IN:(k + 1) * S5_HALF_IN, k * S5_HALF_ST:(k + 1) * S5_HALF_ST]
                                      for w in (wr, wi)], axis=1) for k in range(2)]).astype(BF16)
    cr, ci = out_map(c_re), out_map(c_im)
    cre = jnp.stack([cr[k * S5_HALF_ST:(k + 1) * S5_HALF_ST, k * S5_HALF_IN:(k + 1) * S5_HALF_IN]
                     for k in range(2)]).astype(BF16)
    cim = jnp.stack([ci[k * S5_HALF_ST:(k + 1) * S5_HALF_ST, k * S5_HALF_IN:(k + 1) * S5_HALF_IN]
                     for k in range(2)]).astype(BF16)
    a = jnp.stack([ar.reshape(-1), ai.reshape(-1)])
    return a, win, cre, cim


def _s5_glu(u, h0_re, h0_im, weights, d_skip, w_glu_bf16, steps, batch_major):
    if batch_major:
        nb, seq, _ = u.shape
        blk, idx, out_full = (nb, steps, A_WIDTH), (lambda j: (0, j, 0)), (nb, seq, A_WIDTH)
    else:
        seq, nb, _ = u.shape
        blk, idx, out_full = (steps, nb, A_WIDTH), (lambda j: (j, 0, 0)), (seq, nb, A_WIDTH)
    a, win, cre, cim = weights
    rows = steps * nb
    const = lambda *shape: pl.BlockSpec(shape, lambda j: (0,) * len(shape))
    extra_specs, extra_args = [], []
    if batch_major:
        r = jnp.arange(rows, dtype=jnp.int32)
        src = (r % nb) * steps + r // nb
        perm = (src[:, None] == r[None, :]).astype(BF16)
        extra_specs, extra_args = [const(rows, rows), const(rows, rows)], [perm, perm.T]
    return pl.pallas_call(
        functools.partial(_s5_kernel, nb=nb, steps=steps, batch_major=batch_major),
        grid=(seq // steps,),
        in_specs=[pl.BlockSpec(blk, idx),
                  const(nb, S5_STATES), const(nb, S5_STATES), const(2, S5_STATES),
                  const(2, S5_HALF_IN, 2 * S5_HALF_ST), const(2, S5_HALF_ST, S5_HALF_IN),
                  const(2, S5_HALF_ST, S5_HALF_IN), const(1, A_WIDTH), const(A_WIDTH, A_WIDTH)] + extra_specs,
        out_specs=[pl.BlockSpec(blk, idx), const(nb, S5_STATES), const(nb, S5_STATES)],
        out_shape=[jax.ShapeDtypeStruct(out_full, F32), jax.ShapeDtypeStruct((nb, S5_STATES), F32),
                   jax.ShapeDtypeStruct((nb, S5_STATES), F32)],
        scratch_shapes=[pltpu.VMEM((rows, 2 * S5_STATES), F32), pltpu.VMEM((nb, S5_STATES), F32),
                        pltpu.VMEM((nb, S5_STATES), F32)],
        compiler_params=_cparams(("arbitrary",)), name="s5_glu",
    )(u, h0_re, h0_im, a, win, cre, cim, d_skip.reshape(1, A_WIDTH), w_glu_bf16, *extra_args)


GDN_BA_COLS = 128
HI = lax.Precision.HIGHEST


def _gdn_pre_kernel(x_ref, conv0_ref, ba_ref, w_ref, alog_ref, dtb_ref, q_ref, k_ref, v_ref, bg_ref, xp_sc,
                    *, tm, tmo):
    j = pl.program_id(1)

    @pl.when(j == 0)
    def _():
        xp_sc[0:8, :] = conv0_ref[0]

    @pl.when(j > 0)
    def _():
        xp_sc[0:8, :] = xp_sc[tm:tm + 8, :]

    xp_sc[8:8 + tm, :] = x_ref[0]
    conv = w_ref[0:1, :] * xp_sc[pl.ds(5, tm), :]
    for jj in range(1, C_CONV):
        conv = conv + w_ref[jj:jj + 1, :] * xp_sc[pl.ds(5 + jj, tm), :]
    act = conv * jax.nn.sigmoid(conv)
    pad = tmo - tm

    def put(ref, val):
        if pad:
            val = jnp.concatenate([val, jnp.zeros((pad, val.shape[1]), F32)], axis=0)
        ref[0] = val

    qs, ks = [], []
    for h in range(C_HEADS):
        qh = act[:, h * C_DK:(h + 1) * C_DK]
        kh = act[:, C_QK + h * C_DK:C_QK + (h + 1) * C_DK]
        qs.append(qh * lax.rsqrt(jnp.sum(qh * qh, axis=-1, keepdims=True) + EPS) * (C_DK ** -0.5))
        ks.append(kh * lax.rsqrt(jnp.sum(kh * kh, axis=-1, keepdims=True) + EPS))
    put(q_ref, jnp.concatenate(qs, axis=-1))
    put(k_ref, jnp.concatenate(ks, axis=-1))
    put(v_ref, act[:, 2 * C_QK:])
    ba = ba_ref[0]
    lane = lax.broadcasted_iota(jnp.int32, ba.shape, 1)
    beta = jax.nn.sigmoid(ba)
    g = -jnp.exp(alog_ref[...]) * jax.nn.softplus(ba + dtb_ref[...])
    bg = jnp.where(lane < C_HEADS, beta, jnp.where(lane < 2 * C_HEADS, g, 0.0))
    put(bg_ref, bg)


def _gdn_pre(proj3, conv0, conv_w, a_log, dt_bias, tm, tmo):
    bsz, seq, _ = proj3.shape
    nblk = seq // tm
    conv0_p = jnp.pad(conv0, ((0, 0), (8 - (C_CONV - 1), 0), (0, 0)))
    vec = lambda x: jnp.zeros((1, GDN_BA_COLS), F32).at[0, C_HEADS:2 * C_HEADS].set(x)
    ba_blk = (C_CONV_CH + C_V) // GDN_BA_COLS
    out_len = nblk * tmo
    return pl.pallas_call(
        functools.partial(_gdn_pre_kernel, tm=tm, tmo=tmo),
        grid=(bsz, nblk),
        in_specs=[pl.BlockSpec((1, tm, C_CONV_CH), lambda b, j: (b, j, 0)),
                  pl.BlockSpec((1, 8, C_CONV_CH), lambda b, j: (b, 0, 0)),
                  pl.BlockSpec((1, tm, GDN_BA_COLS), lambda b, j: (b, j, ba_blk)),
                  pl.BlockSpec((C_CONV, C_CONV_CH), lambda b, j: (0, 0)),
                  pl.BlockSpec((1, GDN_BA_COLS), lambda b, j: (0, 0)),
                  pl.BlockSpec((1, GDN_BA_COLS), lambda b, j: (0, 0))],
        out_specs=[pl.BlockSpec((1, tmo, C_QK), lambda b, j: (b, j, 0)),
                   pl.BlockSpec((1, tmo, C_QK), lambda b, j: (b, j, 0)),
                   pl.BlockSpec((1, tmo, C_V), lambda b, j: (b, j, 0)),
                   pl.BlockSpec((1, tmo, GDN_BA_COLS), lambda b, j: (b, j, 0))],
        out_shape=[jax.ShapeDtypeStruct((bsz, out_len, C_QK), F32), jax.ShapeDtypeStruct((bsz, out_len, C_QK), F32),
                   jax.ShapeDtypeStruct((bsz, out_len, C_V), F32),
                   jax.ShapeDtypeStruct((bsz, out_len, GDN_BA_COLS), F32)],
        scratch_shapes=[pltpu.VMEM((tm + 16, C_CONV_CH), F32)],
        compiler_params=_cparams(("parallel", "arbitrary")), name="gdn_pre",
    )(proj3, conv0_p, proj3, conv_w, vec(a_log), vec(dt_bias))


def _dot_hi(a, b):
    return jnp.dot(a, b, preferred_element_type=F32, precision=HI)


def _dot_bf(a, b):
    return jnp.dot(a.astype(BF16), b.astype(BF16), preferred_element_type=F32)


_NN = (((1,), (0,)), ((), ()))
_NT = (((1,), (1,)), ((), ()))


def _split2(x):
    hi = x.astype(BF16)
    return hi, (x - hi.astype(F32)).astype(BF16)


def _dot3(a2, b2, dims=_NN):
    d = lambda x, y: lax.dot_general(x, y, dims, preferred_element_type=F32)
    return d(a2[0], b2[0]) + (d(a2[0], b2[1]) + d(a2[1], b2[0]))


def _dot_sel(sel_bf16, b, dims=_NN):
    b1 = b.astype(BF16)
    r1 = b - b1.astype(F32)
    b2 = r1.astype(BF16)
    b3 = (r1 - b2.astype(F32)).astype(BF16)
    d = lambda y: lax.dot_general(sel_bf16, y, dims, preferred_element_type=F32)
    return d(b1) + (d(b2) + d(b3))


def _gdn_chunk_kernel(q_ref, k_ref, v_ref, bg_ref, z_ref, s0_ref, ng_ref, y_ref, s_out, s_sc, *, chunk, nchunk, zlen):
    j = pl.program_id(1)

    @pl.when(j == 0)
    def _():
        s_sc[...] = s0_ref[0]

    ri = lax.broadcasted_iota(jnp.int32, (chunk, chunk), 0)
    ci = lax.broadcasted_iota(jnp.int32, (chunk, chunk), 1)
    causal = ci <= ri
    strict = ci < ri
    ltri = causal.astype(BF16)
    eye = (ci == ri).astype(F32)
    e0 = (lax.broadcasted_iota(jnp.int32, (chunk, C_DK), 1) == 0).astype(BF16)
    n_dbl = chunk.bit_length() - 2

    def do_chunk(c, carry):
        r = pl.ds(pl.multiple_of(c * chunk, chunk), chunk)
        bg = bg_ref[0, r, :]
        for h in range(C_HEADS):
            q = q_ref[0, r, h * C_DK:(h + 1) * C_DK]
            k = k_ref[0, r, h * C_DK:(h + 1) * C_DK]
            v = v_ref[0, r, h * C_DV:(h + 1) * C_DV]
            beta = bg[:, h:h + 1]
            g = bg[:, C_HEADS + h:C_HEADS + h + 1]
            gcol = _dot_sel(ltri, jnp.broadcast_to(g, (chunk, C_DK)))
            grow = _dot_sel(e0, gcol, _NT)
            decay = jnp.where(causal, jnp.exp(jnp.where(causal, gcol[:, :chunk] - grow, 0.0)), 0.0)
            kb = k * beta
            vb = v * beta
            k_2 = _split2(k)
            m = jnp.where(strict, _dot3(_split2(kb), k_2, _NT) * decay, 0.0)
            tinv = eye - m
            mp_2 = _split2(m)
            for _ in range(n_dbl):
                mp_2 = _split2(_dot3(mp_2, mp_2))
                tinv = tinv + _dot3(_split2(tinv), mp_2)
            tinv_b = tinv.astype(BF16)
            u = _dot_bf(tinv_b, vb)
            w = _dot_bf(tinv_b, kb * jnp.exp(gcol))
            s = s_sc[h]
            s_b = s.astype(BF16)
            v_new = u - _dot_bf(w, s_b)
            a_intra = jnp.where(causal, _dot3(_split2(q), k_2, _NT), 0.0) * decay
            o = _dot_bf(q * jnp.exp(gcol), s_b) + _dot_bf(a_intra, v_new)
            g_last = gcol[chunk - 1:chunk, :]
            k2 = k * jnp.exp(g_last - gcol)
            s_sc[h] = s * jnp.exp(g_last) + lax.dot_general(k2.astype(BF16), v_new.astype(BF16),
                                                            (((0,), (0,)), ((), ())), preferred_element_type=F32)
            o = o * lax.rsqrt(jnp.mean(o * o, axis=-1, keepdims=True) + EPS) * ng_ref[...]
            z = z_ref[0, pl.ds(pl.multiple_of(c * chunk, chunk), zlen), h * C_DV:(h + 1) * C_DV]
            y_ref[0, pl.ds(pl.multiple_of(c * chunk, chunk), zlen), h * C_DV:(h + 1) * C_DV] = (
                o[:zlen] * (z * jax.nn.sigmoid(z)))
        return carry

    lax.fori_loop(0, nchunk, do_chunk, 0)

    @pl.when(j == pl.num_programs(1) - 1)
    def _():
        s_out[0] = s_sc[...]


def _gdn_chunks(q, k, v, bg, proj3, s0, norm_g, chunk, nchunk, zlen):
    bsz, out_len, _ = q.shape
    rb = chunk * nchunk
    nblk = out_len // rb
    zrows = zlen if zlen < chunk else rb
    z_blk = C_CONV_CH // C_V
    row = lambda w: pl.BlockSpec((1, rb, w), lambda b, j: (b, j, 0))
    return pl.pallas_call(
        functools.partial(_gdn_chunk_kernel, chunk=chunk, nchunk=nchunk, zlen=min(zlen, chunk)),
        grid=(bsz, nblk),
        in_specs=[row(C_QK), row(C_QK), row(C_V), row(GDN_BA_COLS),
                  pl.BlockSpec((1, zrows, C_V), lambda b, j: (b, j, z_blk)),
                  pl.BlockSpec((1, C_HEADS, C_DK, C_DV), lambda b, j: (b, 0, 0, 0)),
                  pl.BlockSpec((1, C_DV), lambda b, j: (0, 0))],
        out_specs=[pl.BlockSpec((1, zrows, C_V), lambda b, j: (b, j, 0)),
                   pl.BlockSpec((1, C_HEADS, C_DK, C_DV), lambda b, j: (b, 0, 0, 0))],
        out_shape=[jax.ShapeDtypeStruct((bsz, nblk * zrows, C_V), F32),
                   jax.ShapeDtypeStruct((bsz, C_HEADS, C_DK, C_DV), F32)],
        scratch_shapes=[pltpu.VMEM((C_HEADS, C_DK, C_DV), F32)],
        compiler_params=_cparams(("parallel", "arbitrary")), name="gdn_chunks",
    )(q, k, v, bg, proj3, s0, norm_g.reshape(1, C_DV))


def _tri_pairs(n):
    qi = [i for i in range(n) for _ in range(i + 1)]
    kj = [j for i in range(n) for j in range(i + 1)]
    return jnp.asarray(qi, jnp.int32), jnp.asarray(kj, jnp.int32)


def _softmax_update(s, v_bf16, m_ref, l_ref, acc_ref, idx):
    m_prev = m_ref[idx]
    m_new = jnp.maximum(m_prev, jnp.max(s, axis=-1, keepdims=True))
    corr = jnp.exp(m_prev - m_new)
    p = jnp.exp(s - m_new)
    l_ref[idx] = corr * l_ref[idx] + jnp.sum(p, axis=-1, keepdims=True)
    acc_ref[idx] = corr * acc_ref[idx] + jnp.dot(p.astype(BF16), v_bf16, preferred_element_type=F32)
    m_ref[idx] = m_new


def _diff_attn_kernel(qi_ref, kj_ref, lam_ref, q_ref, k_ref, v_ref, g_ref, o_ref, m_ref, l_ref, acc_ref,
                      *, blk, out_scale):
    p = pl.program_id(2)
    qi = qi_ref[p]
    kj = kj_ref[p]

    @pl.when(kj == 0)
    def _():
        m_ref[...] = jnp.full(m_ref.shape, NEG_BIG, F32)
        l_ref[...] = jnp.zeros(l_ref.shape, F32)
        acc_ref[...] = jnp.zeros(acc_ref.shape, F32)

    def step(masked):
        q = (q_ref[...] * (B_DK ** -0.5)).astype(BF16)
        k = k_ref[...].astype(BF16)
        v = v_ref[...].astype(BF16)
        for c in range(2):
            s = lax.dot_general(q[:, c * B_DK:(c + 1) * B_DK], k[:, c * B_DK:(c + 1) * B_DK],
                                (((1,), (1,)), ((), ())), preferred_element_type=F32)
            if masked:
                row = lax.broadcasted_iota(jnp.int32, s.shape, 0)
                col = lax.broadcasted_iota(jnp.int32, s.shape, 1)
                s = jnp.where(col <= row, s, NEG_BIG)
            _softmax_update(s, v, m_ref, l_ref, acc_ref, c)

    @pl.when(kj < qi)
    def _():
        step(False)

    @pl.when(kj == qi)
    def _():
        step(True)
        lam = lam_ref[0, 0]
        o0 = acc_ref[0] / l_ref[0]
        o1 = acc_ref[1] / l_ref[1]
        y = o0 - lam * o1
        y = y * lax.rsqrt(jnp.mean(y * y, axis=-1, keepdims=True) + EPS)
        o_ref[...] = y * g_ref[...] * out_scale


def _diff_attn_prompt(q, k, v, lam, subln, bsz, seq, lam_init, blk=512):
    nq = seq // blk
    qi, kj = _tri_pairs(nq)
    kern = functools.partial(_diff_attn_kernel, blk=blk, out_scale=1.0 - lam_init)
    grid_spec = pltpu.PrefetchScalarGridSpec(
        num_scalar_prefetch=2,
        grid=(bsz, B_HEADS, qi.shape[0]),
        in_specs=[
            pl.BlockSpec(memory_space=pltpu.SMEM),
            pl.BlockSpec((blk, B_DV), lambda b, h, p, qi, kj: (b * nq + qi[p], h)),
            pl.BlockSpec((blk, B_DV), lambda b, h, p, qi, kj: (b * nq + kj[p], h)),
            pl.BlockSpec((blk, B_DV), lambda b, h, p, qi, kj: (b * nq + kj[p], h)),
            pl.BlockSpec((1, B_DV), lambda b, h, p, qi, kj: (0, 0)),
        ],
        out_specs=pl.BlockSpec((blk, B_DV), lambda b, h, p, qi, kj: (b * nq + qi[p], h)),
        scratch_shapes=[pltpu.VMEM((2, blk, 1), F32), pltpu.VMEM((2, blk, 1), F32),
                        pltpu.VMEM((2, blk, B_DV), F32)],
    )
    return pl.pallas_call(
        kern, grid_spec=grid_spec, out_shape=jax.ShapeDtypeStruct(q.shape, F32),
        compiler_params=_cparams(("parallel", "parallel", "arbitrary")), name="diff_attn_prompt",
    )(qi, kj, lam.reshape(1, 1), q, k, v, subln.reshape(1, B_DV))


def _mla_attn_kernel(qi_ref, kj_ref, ql_ref, qr_ref, c_ref, kr_ref, wuv_ref, o_ref, m_ref, l_ref, acc_ref, *, blk):
    p = pl.program_id(1)
    qi = qi_ref[p]
    kj = kj_ref[p]
    scale = (D_NOPE + D_ROPE) ** -0.5

    @pl.when(kj == 0)
    def _():
        m_ref[...] = jnp.full(m_ref.shape, NEG_BIG, F32)
        l_ref[...] = jnp.zeros(l_ref.shape, F32)
        acc_ref[...] = jnp.zeros(acc_ref.shape, F32)

    def step(masked):
        c = c_ref[...].astype(BF16)
        kr = kr_ref[...].astype(BF16)
        for h in range(D_HEADS):
            ql = (ql_ref[:, h * D_KV_LORA:(h + 1) * D_KV_LORA] * scale).astype(BF16)
            qr = (qr_ref[:, h * D_ROPE:(h + 1) * D_ROPE] * scale).astype(BF16)
            s = (lax.dot_general(ql, c, (((1,), (1,)), ((), ())), preferred_element_type=F32)
                 + lax.dot_general(qr, kr, (((1,), (1,)), ((), ())), preferred_element_type=F32))
            if masked:
                row = lax.broadcasted_iota(jnp.int32, s.shape, 0)
                col = lax.broadcasted_iota(jnp.int32, s.shape, 1)
                s = jnp.where(col <= row, s, NEG_BIG)
            _softmax_update(s, c, m_ref, l_ref, acc_ref, h)

    @pl.when(kj < qi)
    def _():
        step(False)

    @pl.when(kj == qi)
    def _():
        step(True)
        for h in range(D_HEADS):
            o = (acc_ref[h] / l_ref[h]).astype(BF16)
            o_ref[:, h * D_DV:(h + 1) * D_DV] = jnp.dot(o, wuv_ref[h], preferred_element_type=F32)


def _mla_attn_prompt(q_lat, q_rope, c, kr, w_uv_bf16, bsz, seq, blk=512):
    nq = seq // blk
    qi, kj = _tri_pairs(nq)
    grid_spec = pltpu.PrefetchScalarGridSpec(
        num_scalar_prefetch=2,
        grid=(bsz, qi.shape[0]),
        in_specs=[
            pl.BlockSpec((blk, D_HEADS * D_KV_LORA), lambda b, p, qi, kj: (b * nq + qi[p], 0)),
            pl.BlockSpec((blk, D_HEADS * D_ROPE), lambda b, p, qi, kj: (b * nq + qi[p], 0)),
            pl.BlockSpec((blk, D_KV_LORA), lambda b, p, qi, kj: (b * nq + kj[p], 0)),
            pl.BlockSpec((blk, D_ROPE), lambda b, p, qi, kj: (b * nq + kj[p], 0)),
            pl.BlockSpec((D_HEADS, D_KV_LORA, D_DV), lambda b, p, qi, kj: (0, 0, 0)),
        ],
        out_specs=pl.BlockSpec((blk, D_HEADS * D_DV), lambda b, p, qi, kj: (b * nq + qi[p], 0)),
        scratch_shapes=[pltpu.VMEM((D_HEADS, blk, 1), F32), pltpu.VMEM((D_HEADS, blk, 1), F32),
                        pltpu.VMEM((D_HEADS, blk, D_KV_LORA), F32)],
    )
    return pl.pallas_call(
        functools.partial(_mla_attn_kernel, blk=blk), grid_spec=grid_spec,
        out_shape=jax.ShapeDtypeStruct((q_lat.shape[0], D_HEADS * D_DV), F32),
        compiler_params=_cparams(("parallel", "arbitrary")), name="mla_attn_prompt",
    )(qi, kj, q_lat, q_rope, c, kr, w_uv_bf16)


PAGES_PER_STEP = 16
NEW_PAD = 8


def _page_spec(width, i, n_pages, base):
    return pl.BlockSpec((1, PAGE_SIZE, width),
                        lambda b, p, pt: (pt[b * n_pages + p * PAGES_PER_STEP + i] + base, 0, 0))


def _flash_update(s, v_bf16, m_ref, l_ref, acc_ref):
    m_prev = m_ref[...]
    m_new = jnp.maximum(m_prev, jnp.max(s, axis=-1, keepdims=True))
    corr = jnp.exp(m_prev - m_new)
    p = jnp.exp(s - m_new)
    l_ref[...] = corr * l_ref[...] + jnp.sum(p, axis=-1, keepdims=True)
    acc_ref[...] = corr * acc_ref[...] + jnp.dot(p.astype(BF16), v_bf16, preferred_element_type=F32)
    m_ref[...] = m_new


def _flash_update_pages(s, values, m_ref, l_ref, acc_ref):
    m_prev = m_ref[...]
    m_new = jnp.maximum(m_prev, jnp.max(s, axis=-1, keepdims=True))
    corr = jnp.exp(m_prev - m_new)
    p = jnp.exp(s - m_new)
    l_ref[...] = corr * l_ref[...] + jnp.sum(p, axis=-1, keepdims=True)
    pb = p.astype(BF16)
    pv = jnp.dot(pb[:, :PAGE_SIZE], values[0].astype(BF16), preferred_element_type=F32)
    for i in range(1, len(values)):
        pv = pv + jnp.dot(pb[:, i * PAGE_SIZE:(i + 1) * PAGE_SIZE], values[i].astype(BF16),
                          preferred_element_type=F32)
    acc_ref[...] = corr * acc_ref[...] + pv
    m_ref[...] = m_new


def _self_mask(s, t_new):
    row_t = lax.broadcasted_iota(jnp.int32, s.shape, 0) % t_new
    col = lax.broadcasted_iota(jnp.int32, s.shape, 1)
    return jnp.where(col <= row_t, s, NEG_BIG)


def _diff_paged_kernel(pt_ref, lam_ref, q_ref, kn_ref, vn_ref, g_ref, *refs, t_new, out_scale):
    k_refs = refs[:PAGES_PER_STEP]
    v_refs = refs[PAGES_PER_STEP:2 * PAGES_PER_STEP]
    o_ref, m_ref, l_ref, acc_ref = refs[2 * PAGES_PER_STEP:]
    p = pl.program_id(1)

    @pl.when(p == 0)
    def _():
        m_ref[...] = jnp.full(m_ref.shape, NEG_BIG, F32)
        l_ref[...] = jnp.zeros(l_ref.shape, F32)
        acc_ref[...] = jnp.zeros(acc_ref.shape, F32)

    q = (q_ref[0] * (B_DK ** -0.5)).astype(BF16)
    s = jnp.concatenate([lax.dot_general(q, k_refs[i][0].astype(BF16), (((1,), (1,)), ((), ())),
                                         preferred_element_type=F32) for i in range(PAGES_PER_STEP)], axis=-1)
    _flash_update_pages(s, [r[0] for r in v_refs], m_ref, l_ref, acc_ref)

    @pl.when(p == pl.num_programs(1) - 1)
    def _():
        s = lax.dot_general(q, kn_ref[0].astype(BF16), (((1,), (1,)), ((), ())), preferred_element_type=F32)
        _flash_update(_self_mask(s, t_new), vn_ref[0].astype(BF16), m_ref, l_ref, acc_ref)
        o = acc_ref[...] / l_ref[...]
        half = B_HEADS * t_new
        y = o[:half] - lam_ref[0, 0] * o[half:]
        row_h = lax.broadcasted_iota(jnp.int32, y.shape, 0) // t_new
        col_h = lax.broadcasted_iota(jnp.int32, y.shape, 1) // B_DV
        y = jnp.where(row_h == col_h, y, 0.0)
        sel_r = lax.broadcasted_iota(jnp.int32, (NEW_PAD, half), 0)
        sel_c = lax.broadcasted_iota(jnp.int32, (NEW_PAD, half), 1)
        sel = (sel_c % t_new == sel_r).astype(F32)
        y = _dot_hi(sel, y)
        outs = []
        for h in range(B_HEADS):
            yh = y[:, h * B_DV:(h + 1) * B_DV]
            outs.append(yh * lax.rsqrt(jnp.mean(yh * yh, axis=-1, keepdims=True) + EPS) * g_ref[...] * out_scale)
        o_ref[0] = jnp.concatenate(outs, axis=-1)


def _diff_attn_paged(q, k, v, cache_k, cache_v, layer, page_table, lam, subln, lam_init):
    bd, t_new = q.shape[:2]
    n_pool = cache_k.shape[1]
    n_pages = page_table.shape[1]
    ck = cache_k.reshape(-1, PAGE_SIZE, B_HEADS * 2 * B_DK)
    cv = cache_v.reshape(-1, PAGE_SIZE, B_HEADS * B_DV)
    eye_h = jnp.eye(B_HEADS, dtype=F32)
    eye_c = jnp.eye(2, dtype=F32)
    qbd = jnp.einsum('bthcd,hg,ce->bchtged', q, eye_h, eye_c).reshape(bd, 2 * B_HEADS * t_new, B_HEADS * 2 * B_DK)
    padn = lambda x: jnp.pad(x.reshape(bd, t_new, -1), ((0, 0), (0, NEW_PAD - t_new), (0, 0)))
    rows = 2 * B_HEADS * t_new
    width = B_HEADS * B_DV
    per_b = lambda r, w: pl.BlockSpec((1, r, w), lambda b, p, pt: (b, 0, 0))
    grid_spec = pltpu.PrefetchScalarGridSpec(
        num_scalar_prefetch=1,
        grid=(bd, n_pages // PAGES_PER_STEP),
        in_specs=[pl.BlockSpec(memory_space=pltpu.SMEM), per_b(rows, width), per_b(NEW_PAD, width),
                  per_b(NEW_PAD, width), pl.BlockSpec((1, B_DV), lambda b, p, pt: (0, 0))]
        + [_page_spec(width, i, n_pages, layer * n_pool) for i in range(PAGES_PER_STEP)] * 2,
        out_specs=per_b(NEW_PAD, width),
        scratch_shapes=[pltpu.VMEM((rows, 1), F32), pltpu.VMEM((rows, 1), F32), pltpu.VMEM((rows, width), F32)],
    )
    y = pl.pallas_call(
        functools.partial(_diff_paged_kernel, t_new=t_new, out_scale=1.0 - lam_init), grid_spec=grid_spec,
        out_shape=jax.ShapeDtypeStruct((bd, NEW_PAD, width), F32),
        compiler_params=_cparams(("parallel", "arbitrary")), name="diff_attn_paged",
    )(page_table.reshape(-1), lam.reshape(1, 1), qbd, padn(k), padn(v), subln.reshape(1, B_DV),
      *([ck] * PAGES_PER_STEP), *([cv] * PAGES_PER_STEP))
    return y[:, :t_new]


def _mla_paged_kernel(pt_ref, ql_ref, qr_ref, cn_ref, rn_ref, *refs, t_new):
    c_refs = refs[:PAGES_PER_STEP]
    r_refs = refs[PAGES_PER_STEP:2 * PAGES_PER_STEP]
    o_ref, m_ref, l_ref, acc_ref = refs[2 * PAGES_PER_STEP:]
    p = pl.program_id(1)
    scale = (D_NOPE + D_ROPE) ** -0.5

    @pl.when(p == 0)
    def _():
        m_ref[...] = jnp.full(m_ref.shape, NEG_BIG, F32)
        l_ref[...] = jnp.zeros(l_ref.shape, F32)
        acc_ref[...] = jnp.zeros(acc_ref.shape, F32)

    ql = (ql_ref[0] * scale).astype(BF16)
    qr = (qr_ref[0] * scale).astype(BF16)
    nt = (((1,), (1,)), ((), ()))

    def scores(c_bf16, r_bf16):
        return (lax.dot_general(ql, c_bf16, nt, preferred_element_type=F32)
                + lax.dot_general(qr, r_bf16, nt, preferred_element_type=F32))

    cs = [c_refs[i][0].astype(BF16) for i in range(PAGES_PER_STEP)]
    s = jnp.concatenate([scores(cs[i], r_refs[i][0].astype(BF16)) for i in range(PAGES_PER_STEP)], axis=-1)
    _flash_update_pages(s, cs, m_ref, l_ref, acc_ref)

    @pl.when(p == pl.num_programs(1) - 1)
    def _():
        c = cn_ref[0].astype(BF16)
        _flash_update(_self_mask(scores(c, rn_ref[0].astype(BF16)), t_new), c, m_ref, l_ref, acc_ref)
        o_ref[0] = acc_ref[...] / l_ref[...]


def _mla_attn_paged(q_lat, q_rope, c, kr, cache_lat, cache_rope, layer, page_table):
    bd, t_new = q_lat.shape[:2]
    n_pool = cache_lat.shape[1]
    n_pages = page_table.shape[1]
    cl = cache_lat.reshape(-1, PAGE_SIZE, D_KV_LORA)
    cr = cache_rope.reshape(-1, PAGE_SIZE, D_ROPE)
    rows = D_HEADS * t_new
    ql = q_lat.transpose(0, 2, 1, 3).reshape(bd, rows, D_KV_LORA)
    qr = q_rope.transpose(0, 2, 1, 3).reshape(bd, rows, D_ROPE)
    padn = lambda x: jnp.pad(x, ((0, 0), (0, NEW_PAD - t_new), (0, 0)))
    per_b = lambda r, w: pl.BlockSpec((1, r, w), lambda b, p, pt: (b, 0, 0))
    grid_spec = pltpu.PrefetchScalarGridSpec(
        num_scalar_prefetch=1,
        grid=(bd, n_pages // PAGES_PER_STEP),
        in_specs=[per_b(rows, D_KV_LORA), per_b(rows, D_ROPE), per_b(NEW_PAD, D_KV_LORA), per_b(NEW_PAD, D_ROPE)]
        + [_page_spec(D_KV_LORA, i, n_pages, layer * n_pool) for i in range(PAGES_PER_STEP)]
        + [_page_spec(D_ROPE, i, n_pages, layer * n_pool) for i in range(PAGES_PER_STEP)],
        out_specs=per_b(rows, D_KV_LORA),
        scratch_shapes=[pltpu.VMEM((rows, 1), F32), pltpu.VMEM((rows, 1), F32), pltpu.VMEM((rows, D_KV_LORA), F32)],
    )
    o = pl.pallas_call(
        functools.partial(_mla_paged_kernel, t_new=t_new), grid_spec=grid_spec,
        out_shape=jax.ShapeDtypeStruct((bd, rows, D_KV_LORA), F32),
        compiler_params=_cparams(("parallel", "arbitrary")), name="mla_attn_paged",
    )(page_table.reshape(-1), ql, qr, padn(c), padn(kr), *([cl] * PAGES_PER_STEP), *([cr] * PAGES_PER_STEP))
    return o.reshape(bd, D_HEADS, t_new, D_KV_LORA).transpose(0, 2, 1, 3)


ROUTER_PAD = 128


def _out_ln_router_kernel(xa_ref, xb_ref, wa_ref, wb_ref, h_ref, g_ref, b_ref, rw_ref, rb_ref, o_ref, lg_ref):
    mix = (jnp.dot(xa_ref[...].astype(BF16), wa_ref[...], preferred_element_type=F32)
           + jnp.dot(xb_ref[...].astype(BF16), wb_ref[...], preferred_element_type=F32))
    x = DN_ALPHA * h_ref[...] + mix
    mu = jnp.mean(x, axis=-1, keepdims=True)
    xc = x - mu
    var = jnp.mean(xc * xc, axis=-1, keepdims=True)
    y = xc * lax.rsqrt(var + EPS) * g_ref[...] + b_ref[...]
    o_ref[...] = y
    lg_ref[...] = _dot_hi(y, rw_ref[...]) + rb_ref[...]


def _out_ln_router(xa, xb, w_out, h, ln_g, ln_b, router_w, router_b, tm=256):
    n, d = h.shape
    ka, kb = xa.shape[1], xb.shape[1]
    tm = min(tm, n)
    rw = jnp.pad(router_w, ((0, 0), (0, ROUTER_PAD - N_EXPERTS)))
    rb = jnp.pad(router_b, (0, ROUTER_PAD - N_EXPERTS)).reshape(1, ROUTER_PAD)
    row = lambda w: pl.BlockSpec((tm, w), lambda i: (i, 0))
    const = lambda r, w: pl.BlockSpec((r, w), lambda i: (0, 0))
    return pl.pallas_call(
        _out_ln_router_kernel, grid=(n // tm,),
        in_specs=[row(ka), row(kb), const(ka, d), const(kb, d), row(d), const(1, d), const(1, d),
                  const(d, ROUTER_PAD), const(1, ROUTER_PAD)],
        out_specs=[row(d), row(ROUTER_PAD)],
        out_shape=[jax.ShapeDtypeStruct((n, d), F32), jax.ShapeDtypeStruct((n, ROUTER_PAD), F32)],
        compiler_params=_cparams(("parallel",)), name="out_ln_router",
    )(xa, xb, w_out[:ka].astype(BF16), w_out[ka:].astype(BF16), h, ln_g.reshape(1, d), ln_b.reshape(1, d), rw, rb)


def _combine_ln_kernel(*refs):
    ya_refs = refs[:TOP_K]
    gt_ref, h_ref, g_ref, b_ref, o_ref = refs[TOP_K:]
    gates = gt_ref[...]
    f = gates[:, 0:1] * ya_refs[0][...]
    for kk in range(1, TOP_K):
        f = f + gates[:, kk:kk + 1] * ya_refs[kk][...]
    x = DN_ALPHA * h_ref[...] + f
    mu = jnp.mean(x, axis=-1, keepdims=True)
    xc = x - mu
    var = jnp.mean(xc * xc, axis=-1, keepdims=True)
    o_ref[...] = xc * lax.rsqrt(var + EPS) * g_ref[...] + b_ref[...]


def _combine_ln(y_assign, gates, h, ln_g, ln_b, tm=256):
    n, d = h.shape
    tm = min(tm, n)
    nblk = n // tm
    row = lambda w: pl.BlockSpec((tm, w), lambda i: (i, 0))
    const = lambda r, w: pl.BlockSpec((r, w), lambda i: (0, 0))
    choice = lambda kk: pl.BlockSpec((tm, d), lambda i: (kk * nblk + i, 0))
    return pl.pallas_call(
        _combine_ln_kernel, grid=(nblk,),
        in_specs=[choice(kk) for kk in range(TOP_K)] + [row(TOP_K), row(d), const(1, d), const(1, d)],
        out_specs=row(d), out_shape=jax.ShapeDtypeStruct((n, d), F32),
        compiler_params=_cparams(("parallel",)), name="combine_ln",
    )(*([y_assign] * TOP_K), gates, h, ln_g.reshape(1, d), ln_b.reshape(1, d))


def _moe_kernel(be_ref, nu_ref, x_ref, wgu_ref, bgu_ref, wd_ref, bd_ref, o_ref, wgu_bf, wd_bf):
    i = pl.program_id(0)
    e = be_ref[i]
    e_prev = be_ref[jnp.maximum(i - 1, 0)]

    @pl.when((i == 0) | (e != e_prev))
    def _():
        wgu_bf[...] = wgu_ref[0].astype(BF16)
        wd_bf[...] = wd_ref[0].astype(BF16)

    @pl.when(i < nu_ref[0])
    def _():
        h = jnp.dot(x_ref[...].astype(BF16), wgu_bf[...], preferred_element_type=F32) + bgu_ref[0]
        gate = jnp.minimum(h[:, :D_FF], SWIGLU_LIMIT)
        up = jnp.clip(h[:, D_FF:], -SWIGLU_LIMIT, SWIGLU_LIMIT)
        act = (up + 1.0) * gate * jax.nn.sigmoid(SWIGLU_ALPHA * gate)
        o_ref[...] = jnp.dot(act.astype(BF16), wd_bf[...], preferred_element_type=F32) + bd_ref[0]

    @pl.when(i >= nu_ref[0])
    def _():
        o_ref[...] = jnp.zeros(o_ref.shape, F32)


def _moe_experts(x_sorted, block_e, n_used, w_gu, b_gu, w_down, b_down, blk):
    p_rows, d = x_sorted.shape
    n_blocks = p_rows // blk
    grid_spec = pltpu.PrefetchScalarGridSpec(
        num_scalar_prefetch=2,
        grid=(n_blocks,),
        in_specs=[
            pl.BlockSpec((blk, d), lambda i, be, nu: (i, 0)),
            pl.BlockSpec((1, d, 2 * D_FF), lambda i, be, nu: (be[i], 0, 0)),
            pl.BlockSpec((1, 1, 2 * D_FF), lambda i, be, nu: (be[i], 0, 0)),
            pl.BlockSpec((1, D_FF, d), lambda i, be, nu: (be[i], 0, 0)),
            pl.BlockSpec((1, 1, d), lambda i, be, nu: (be[i], 0, 0)),
        ],
        out_specs=pl.BlockSpec((blk, d), lambda i, be, nu: (i, 0)),
        scratch_shapes=[pltpu.VMEM((d, 2 * D_FF), BF16), pltpu.VMEM((D_FF, d), BF16)],
    )
    return pl.pallas_call(
        _moe_kernel, grid_spec=grid_spec, out_shape=jax.ShapeDtypeStruct((p_rows, d), F32),
        compiler_params=_cparams(("arbitrary",)), name="moe_experts",
    )(block_e, n_used, x_sorted, w_gu, b_gu.reshape(N_EXPERTS, 1, 2 * D_FF), w_down,
      b_down.reshape(N_EXPERTS, 1, d))


def _moe(xt, logits, w_gu, b_gu, w_down, b_down, ln_g, ln_b):
    n, d = xt.shape
    blk = 256 if n >= 8192 else 128
    top_val, top_idx = lax.top_k(logits, TOP_K)
    gates = jax.nn.softmax(top_val, axis=-1)
    nk = n * TOP_K
    flat_e = top_idx.reshape(-1).astype(jnp.int32)
    onehot = (flat_e[:, None] == jnp.arange(N_EXPERTS, dtype=jnp.int32)[None, :]).astype(jnp.int32)
    csum = jnp.cumsum(onehot, axis=0)
    counts = csum[-1]
    rank = jnp.sum((csum - onehot) * onehot, axis=1)
    padded = (counts + blk - 1) // blk * blk
    pend = jnp.cumsum(padded)
    pstart = pend - padded
    dest = pstart[flat_e] + rank
    n_blocks = -(-(nk + N_EXPERTS * (blk - 1)) // blk)
    p_rows = n_blocks * blk
    buf_tok = jnp.zeros((p_rows,), jnp.int32).at[dest].set(jnp.arange(nk, dtype=jnp.int32) // TOP_K,
                                                            unique_indices=True)
    block_e = jnp.minimum(jnp.searchsorted(pend, jnp.arange(n_blocks, dtype=jnp.int32) * blk, side='right'),
                          N_EXPERTS - 1).astype(jnp.int32)
    n_used = (pend[-1] // blk).astype(jnp.int32).reshape(1)
    x_sorted = xt[buf_tok]
    y_buf = _moe_experts(x_sorted, block_e, n_used, w_gu, b_gu, w_down, b_down, blk)
    y_assign = y_buf[dest.reshape(n, TOP_K).T.reshape(-1)]
    return _combine_ln(y_assign, gates, xt, ln_g, ln_b)


def _rms_norm(x, g):
    return x * lax.rsqrt(jnp.mean(x * x, axis=-1, keepdims=True) + EPS) * g


def _rope(x, pos, theta, n_rot):
    half = n_rot // 2
    inv = jnp.exp(-math.log(theta) * jnp.arange(half, dtype=F32) * (2.0 / n_rot))
    ang = pos[:, None] * inv[None, :]
    cos = jnp.cos(ang)[:, None, :]
    sin = jnp.sin(ang)[:, None, :]
    x1 = x[..., :half]
    x2 = x[..., half:n_rot]
    return jnp.concatenate([x1 * cos - x2 * sin, x2 * cos + x1 * sin, x[..., n_rot:]], axis=-1)


def _even_mixer(x, pos, h0_re, h0_im, sample_ctx, lam_init, w_in, lam_re, lam_im, log_dt, b_re, b_im,
                c_re, c_im, d_skip, w_glu, lq1, lk1, lq2, lk2, subln, w_out):
    bsz, L, _ = x.shape
    n = bsz * L
    proj = _mm(x.reshape(n, D_MODEL), w_in.astype(BF16))
    batch_major = L % 8 == 0
    q = proj[:, A_WIDTH:A_WIDTH + B_QK]
    k = proj[:, A_WIDTH + B_QK:A_WIDTH + 2 * B_QK]
    v = proj[:, A_WIDTH + 2 * B_QK:]
    s5w = _s5_weights(lam_re, lam_im, log_dt, b_re, b_im, c_re, c_im)
    proj3 = proj.reshape(bsz, L, proj.shape[1])
    u_in = proj3 if batch_major else proj3[:, :, :A_WIDTH].transpose(1, 0, 2)
    y_s5, h_re, h_im = _s5_glu(u_in, h0_re.reshape(bsz, S5_STATES), h0_im.reshape(bsz, S5_STATES), s5w,
                               d_skip, w_glu.astype(BF16), steps=min(L, 16), batch_major=batch_major)
    y_a = (y_s5 if batch_major else y_s5.transpose(1, 0, 2)).reshape(n, A_WIDTH)
    h_re = h_re.reshape(bsz, A_GROUPS, A_STATE)
    h_im = h_im.reshape(bsz, A_GROUPS, A_STATE)
    q = _rope(q.reshape(bsz, L, 2 * B_HEADS, B_DK), pos, ROPE_THETA, B_ROT)
    k = _rope(k.reshape(bsz, L, 2 * B_HEADS, B_DK), pos, ROPE_THETA, B_ROT)
    lam = jnp.exp(jnp.sum(lq1 * lk1)) - jnp.exp(jnp.sum(lq2 * lk2)) + lam_init
    if sample_ctx is None:
        y_b = _diff_attn_prompt(q.reshape(n, B_QK), k.reshape(n, B_QK), v, lam, subln, bsz, L, lam_init)
    else:
        cache_k, cache_v, layer, page_table = sample_ctx
        y_b = _diff_attn_paged(q.reshape(bsz, L, B_HEADS, 2, B_DK), k.reshape(bsz, L, B_HEADS, 2, B_DK),
                               v.reshape(bsz, L, B_HEADS, B_DV), cache_k, cache_v, layer, page_table, lam, subln,
                               lam_init).reshape(n, B_HEADS * B_DV)
    return ((y_a, y_b, w_out), h_re, h_im, k.reshape(bsz, L, B_HEADS, 2 * B_DK),
            v.reshape(bsz, L, B_HEADS, B_DV))


def _odd_mixer(x, pos, conv0, s0, sample_ctx, w_in, conv_w, a_log, dt_bias, gdn_g,
               q_norm, w_uq, kv_norm, w_uk, w_uv, w_out):
    bsz, L, _ = x.shape
    n = bsz * L
    zcols = lambda w: jnp.zeros((D_MODEL, w), F32)
    ba_end = C_CONV_CH + C_V + 2 * C_HEADS
    w_in_p = jnp.concatenate([w_in[:, :ba_end], zcols(GDN_BA_COLS - 2 * C_HEADS), w_in[:, ba_end:],
                              zcols(128 - D_ROPE)], axis=1).astype(BF16)
    o_cq = C_CONV_CH + C_V + GDN_BA_COLS
    o_ckv = o_cq + D_Q_LORA
    o_kr = o_ckv + D_KV_LORA
    proj = _mm(x.reshape(n, D_MODEL), w_in_p)
    proj3 = proj.reshape(bsz, L, proj.shape[1])
    if L >= C_CHUNK:
        tm, tmo, chunk, nchunk, zlen = 512, 512, C_CHUNK, 512 // C_CHUNK, C_CHUNK
    else:
        tm, tmo, chunk, nchunk, zlen = L, 8, 8, 1, L
    gq, gk, gv, gbg = _gdn_pre(proj3, conv0, conv_w, a_log, dt_bias, tm, tmo)
    y_c, S = _gdn_chunks(gq, gk, gv, gbg, proj3, s0, gdn_g, chunk, nchunk, zlen)
    conv_new = proj3[:, L - (C_CONV - 1):, :C_CONV_CH]
    c_q = proj3[..., o_cq:o_ckv]
    c_kv = proj3[..., o_ckv:o_kr]
    k_r = proj3[..., o_kr:o_kr + D_ROPE]
    qf = _mm(_rms_norm(c_q, q_norm).reshape(n, D_Q_LORA), w_uq.astype(BF16))
    qf = qf.reshape(bsz, L, D_HEADS, D_NOPE + D_ROPE)
    q_rope = _rope(qf[..., D_NOPE:], pos, MLA_ROPE_THETA, D_ROPE)
    q_nope = qf[..., :D_NOPE].reshape(n, D_HEADS * D_NOPE)
    w_bd = jnp.zeros((D_HEADS * D_NOPE, D_HEADS * D_KV_LORA), F32)
    for h in range(D_HEADS):
        w_bd = w_bd.at[h * D_NOPE:(h + 1) * D_NOPE, h * D_KV_LORA:(h + 1) * D_KV_LORA].set(w_uk[h])
    q_lat = _mm(q_nope, w_bd.astype(BF16))
    c = _rms_norm(c_kv, kv_norm)
    kr = _rope(k_r[:, :, None, :], pos, MLA_ROPE_THETA, D_ROPE)[:, :, 0, :]
    if sample_ctx is None:
        y_d = _mla_attn_prompt(q_lat, q_rope.reshape(n, D_HEADS * D_ROPE), c.reshape(n, D_KV_LORA),
                               kr.reshape(n, D_ROPE), w_uv.astype(BF16), bsz, L)
    else:
        cache_lat, cache_rope, layer, page_table = sample_ctx
        o_lat = _mla_attn_paged(q_lat.reshape(bsz, L, D_HEADS, D_KV_LORA), q_rope, c, kr, cache_lat, cache_rope,
                                layer, page_table)
        w_uv_bd = jnp.zeros((D_HEADS * D_KV_LORA, D_HEADS * D_DV), F32)
        for h in range(D_HEADS):
            w_uv_bd = w_uv_bd.at[h * D_KV_LORA:(h + 1) * D_KV_LORA, h * D_DV:(h + 1) * D_DV].set(w_uv[h])
        y_d = _mm(o_lat.reshape(n, D_HEADS * D_KV_LORA), w_uv_bd.astype(BF16))
    return (y_c.reshape(n, C_V), y_d, w_out), S, conv_new, c, kr


def _post_block(h, mix, li, router_w, router_b, moe_w_gu, moe_b_gu, moe_w_down, moe_b_down, ln_g, ln_b):
    bsz, L, d = h.shape
    xa, xb, w_out = mix
    h1, logits = _out_ln_router(xa, xb, w_out, h.reshape(-1, d), ln_g[li, 0], ln_b[li, 0], router_w[li],
                                router_b[li])
    h2 = _moe(h1, logits[:, :N_EXPERTS], moe_w_gu[li], moe_b_gu[li], moe_w_down[li], moe_b_down[li],
              ln_g[li, 1], ln_b[li, 1])
    return h2.reshape(bsz, L, d)


def kernel(x_prompt, x_sample, state_a_re, state_a_im, cache_b_k, cache_b_v, state_c, state_c_conv,
           cache_d_latent, cache_d_rope, page_table, w_in_even, s5_lam_re, s5_lam_im, s5_log_dt,
           s5_b_re, s5_b_im, s5_c_re, s5_c_im, s5_d, s5_w_glu, diff_lq1, diff_lk1, diff_lq2, diff_lk2,
           diff_subln, w_out_even, w_in_odd, gdn_conv_w, gdn_a_log, gdn_dt_bias, gdn_norm, mla_q_norm,
           mla_w_uq, mla_kv_norm, mla_w_uk, mla_w_uv, w_out_odd, router_w, router_b, moe_w_gu, moe_b_gu,
           moe_w_down, moe_b_down, ln_g, ln_b):
    bp, lp, _ = x_prompt.shape
    bs, ls, _ = x_sample.shape
    pos_p = jnp.arange(lp, dtype=F32)
    pos_s = PAST_LEN + jnp.arange(ls, dtype=F32)
    hp, hs = x_prompt, x_sample
    outs_p = {k: [] for k in ('a_re', 'a_im', 'b_k', 'b_v', 'c', 'c_conv', 'd_lat', 'd_rope')}
    outs_s = {k: [] for k in outs_p}
    for li in range(DEPTH):
        if li % 2 == 0:
            e = li // 2
            lam_init = 0.8 - 0.6 * math.exp(-0.3 * li)
            ew = (w_in_even[e], s5_lam_re[e], s5_lam_im[e], s5_log_dt[e], s5_b_re[e], s5_b_im[e],
                  s5_c_re[e], s5_c_im[e], s5_d[e], s5_w_glu[e], diff_lq1[e], diff_lk1[e], diff_lq2[e],
                  diff_lk2[e], diff_subln[e], w_out_even[e])
            z0 = jnp.zeros((bp, A_GROUPS, A_STATE), F32)
            mp, hr, hi, kk, vv = _even_mixer(hp, pos_p, z0, z0, None, lam_init, *ew)
            for key, val in zip(('a_re', 'a_im', 'b_k', 'b_v'), (hr, hi, kk, vv)):
                outs_p[key].append(val)
            ctx = (cache_b_k, cache_b_v, e, page_table)
            ms, hr, hi, kk, vv = _even_mixer(hs, pos_s, state_a_re[e], state_a_im[e], ctx, lam_init, *ew)
            for key, val in zip(('a_re', 'a_im', 'b_k', 'b_v'), (hr, hi, kk, vv)):
                outs_s[key].append(val)
        else:
            o = li // 2
            ow = (w_in_odd[o], gdn_conv_w[o], gdn_a_log[o], gdn_dt_bias[o], gdn_norm[o], mla_q_norm[o],
                  mla_w_uq[o], mla_kv_norm[o], mla_w_uk[o], mla_w_uv[o], w_out_odd[o])
            conv0 = jnp.zeros((bp, C_CONV - 1, C_CONV_CH), F32)
            s0 = jnp.zeros((bp, C_HEADS, C_DK, C_DV), F32)
            mp, S, cb, cl, kr = _odd_mixer(hp, pos_p, conv0, s0, None, *ow)
            for key, val in zip(('c', 'c_conv', 'd_lat', 'd_rope'), (S, cb, cl, kr)):
                outs_p[key].append(val)
            ctx = (cache_d_latent, cache_d_rope, o, page_table)
            ms, S, cb, cl, kr = _odd_mixer(hs, pos_s, state_c_conv[o], state_c[o], ctx, *ow)
            for key, val in zip(('c', 'c_conv', 'd_lat', 'd_rope'), (S, cb, cl, kr)):
                outs_s[key].append(val)
        post = (li, router_w, router_b, moe_w_gu, moe_b_gu, moe_w_down, moe_b_down, ln_g, ln_b)
        hp = _post_block(hp, mp, *post)
        hs = _post_block(hs, ms, *post)
    keys = ('a_re', 'a_im', 'b_k', 'b_v', 'c', 'c_conv', 'd_lat', 'd_rope')
    return ((hp, hs) + tuple(jnp.stack(outs_p[k]) for k in keys) + tuple(jnp.stack(outs_s[k]) for k in keys))
```

```python
import functools
import math

import jax
import jax.numpy as jnp
from jax import lax
from jax.experimental import pallas as pl
from jax.experimental.pallas import tpu as pltpu

D_MODEL = 1024
DEPTH = 2
PAST_LEN = 16384
PAGE_SIZE = 128
A_WIDTH = D_MODEL // 2
A_GROUP = 16
A_GROUPS = A_WIDTH // A_GROUP
A_STATE = 64
B_HEADS = 4
B_DK = 64
B_DV = 2 * B_DK
B_ROT = B_DK // 4
B_QK = B_HEADS * 2 * B_DK
C_HEADS = 4
C_DK = 128
C_DV = 128
C_CONV = 4
C_CHUNK = 64
C_QK = C_HEADS * C_DK
C_V = C_HEADS * C_DV
C_CONV_CH = 2 * C_QK + C_V
C_IN = C_CONV_CH + C_V + 2 * C_HEADS
D_HEADS = 4
D_NOPE = 128
D_ROPE = 32
D_DV = 128
D_Q_LORA = 384
D_KV_LORA = 256
ROPE_THETA = 500000.0
MLA_ROPE_THETA = 10000.0
N_EXPERTS = 32
TOP_K = 4
D_FF = 1024
SWIGLU_LIMIT = 7.0
SWIGLU_ALPHA = 1.702
DN_ALPHA = (2 * DEPTH) ** 0.25
EPS = 1e-6
NEG_BIG = -1e30

V7X_VMEM_BYTES = 64 * 1024 * 1024
VMEM_LIMIT = V7X_VMEM_BYTES * 7 // 8
BF16 = jnp.bfloat16
F32 = jnp.float32


def _cparams(sem):
    return pltpu.CompilerParams(dimension_semantics=sem, vmem_limit_bytes=VMEM_LIMIT)


def _mm_kernel(x_ref, w_ref, o_ref):
    o_ref[...] = jnp.dot(x_ref[...].astype(BF16), w_ref[...], preferred_element_type=F32)


def _mm(x, w_bf16, tm=256):
    m, k = x.shape
    n = w_bf16.shape[1]
    tm = min(tm, m)
    return pl.pallas_call(
        _mm_kernel,
        grid=(m // tm,),
        in_specs=[pl.BlockSpec((tm, k), lambda i: (i, 0)), pl.BlockSpec((k, n), lambda i: (0, 0))],
        out_specs=pl.BlockSpec((tm, n), lambda i: (i, 0)),
        out_shape=jax.ShapeDtypeStruct((m, n), F32),
        compiler_params=_cparams(("parallel",)),
        name="mm",
    )(x, w_bf16)


S5_HALF_IN = A_WIDTH // 2
S5_HALF_ST = A_GROUPS * A_STATE // 2
S5_STATES = A_GROUPS * A_STATE


def _s5_kernel(u_ref, h0r_ref, h0i_ref, a_ref, win_ref, cre_ref, cim_ref, d_ref, wglu_ref, *rest, nb, steps,
               batch_major):
    if batch_major:
        perm_ref, perm_t_ref = rest[:2]
        rest = rest[2:]
    y_ref, hr_out, hi_out, x_sc, hr_sc, hi_sc = rest
    j = pl.program_id(0)

    @pl.when(j == 0)
    def _():
        hr_sc[...] = h0r_ref[...]
        hi_sc[...] = h0i_ref[...]

    rows = steps * nb
    u = u_ref[...].reshape(rows, A_WIDTH)
    ub = u.astype(BF16)
    if batch_major:
        ub = jnp.dot(perm_ref[...], ub, preferred_element_type=F32).astype(BF16)
    for k in range(2):
        xk = jnp.dot(ub[:, k * S5_HALF_IN:(k + 1) * S5_HALF_IN], win_ref[k], preferred_element_type=F32)
        x_sc[:, k * S5_HALF_ST:(k + 1) * S5_HALF_ST] = xk[:, :S5_HALF_ST]
        x_sc[:, S5_STATES + k * S5_HALF_ST:S5_STATES + (k + 1) * S5_HALF_ST] = xk[:, S5_HALF_ST:]
    ar = a_ref[0:1, :]
    ai = a_ref[1:2, :]

    def body(t, carry):
        r = pl.ds(pl.multiple_of(t * nb, nb), nb)
        hr = hr_sc[...]
        hi = hi_sc[...]
        nr = ar * hr - ai * hi + x_sc[r, 0:S5_STATES]
        ni = ar * hi + ai * hr + x_sc[r, S5_STATES:2 * S5_STATES]
        x_sc[r, 0:S5_STATES] = nr
        x_sc[r, S5_STATES:2 * S5_STATES] = ni
        hr_sc[...] = nr
        hi_sc[...] = ni
        return carry

    lax.fori_loop(0, steps, body, 0)
    ys = []
    for k in range(2):
        hk_r = x_sc[:, k * S5_HALF_ST:(k + 1) * S5_HALF_ST].astype(BF16)
        hk_i = x_sc[:, S5_STATES + k * S5_HALF_ST:S5_STATES + (k + 1) * S5_HALF_ST].astype(BF16)
        yk = (jnp.dot(hk_r, cre_ref[k], preferred_element_type=F32)
              - jnp.dot(hk_i, cim_ref[k], preferred_element_type=F32))
        ys.append(yk)
    y = jnp.concatenate(ys, axis=-1)
    if batch_major:
        y = _dot_sel(perm_t_ref[...], y)
    y = y + d_ref[...] * u
    y = jax.nn.gelu(y)
    y = y * jax.nn.sigmoid(jnp.dot(y.astype(BF16), wglu_ref[...], preferred_element_type=F32))
    y_ref[...] = y.reshape(y_ref.shape)

    @pl.when(j == pl.num_programs(0) - 1)
    def _():
        hr_out[...] = hr_sc[...]
        hi_out[...] = hi_sc[...]


def _s5_weights(lam_re, lam_im, log_dt, b_re, b_im, c_re, c_im):
    dt = jnp.exp(log_dt)[:, None]
    mag = jnp.exp(lam_re * dt)
    ar = mag * jnp.cos(lam_im * dt)
    ai = mag * jnp.sin(lam_im * dt)
    den = lam_re * lam_re + lam_im * lam_im
    fr = ((ar - 1.0) * lam_re + ai * lam_im) / den
    fi = (ai * lam_re - (ar - 1.0) * lam_im) / den
    bbr = fr[..., None] * b_re - fi[..., None] * b_im
    bbi = fr[..., None] * b_im + fi[..., None] * b_re
    eye = jnp.eye(A_GROUPS, dtype=F32)

    def in_map(bb):
        return jnp.einsum('gh,gnp->gphn', eye, bb).reshape(A_WIDTH, S5_STATES)

    def out_map(c):
        return jnp.einsum('gh,gpn->gnhp', eye, c).reshape(S5_STATES, A_WIDTH)

    wr, wi = in_map(bbr), in_map(bbi)
    win = jnp.stack([jnp.concatenate([w[k * S5_HALF_IN:(k + 1) * S5_HALF_IN, k * S5_HALF_ST:(k + 1) * S5_HALF_ST]
                                      for w in (wr, wi)], axis=1) for k in range(2)]).astype(BF16)
    cr, ci = out_map(c_re), out_map(c_im)
    cre = jnp.stack([cr[k * S5_HALF_ST:(k + 1) * S5_HALF_ST, k * S5_HALF_IN:(k + 1) * S5_HALF_IN]
                     for k in range(2)]).astype(BF16)
    cim = jnp.stack([ci[k * S5_HALF_ST:(k + 1) * S5_HALF_ST, k * S5_HALF_IN:(k + 1) * S5_HALF_IN]
                     for k in range(2)]).astype(BF16)
    a = jnp.stack([ar.reshape(-1), ai.reshape(-1)])
    return a, win, cre, cim


def _s5_glu(u, h0_re, h0_im, weights, d_skip, w_glu_bf16, steps, batch_major):
    if batch_major:
        nb, seq, _ = u.shape
        blk, idx, out_full = (nb, steps, A_WIDTH), (lambda j: (0, j, 0)), (nb, seq, A_WIDTH)
    else:
        seq, nb, _ = u.shape
        blk, idx, out_full = (steps, nb, A_WIDTH), (lambda j: (j, 0, 0)), (seq, nb, A_WIDTH)
    a, win, cre, cim = weights
    rows = steps * nb
    const = lambda *shape: pl.BlockSpec(shape, lambda j: (0,) * len(shape))
    extra_specs, extra_args = [], []
    if batch_major:
        r = jnp.arange(rows, dtype=jnp.int32)
        src = (r % nb) * steps + r // nb
        perm = (src[:, None] == r[None, :]).astype(BF16)
        extra_specs, extra_args = [const(rows, rows), const(rows, rows)], [perm, perm.T]
    return pl.pallas_call(
        functools.partial(_s5_kernel, nb=nb, steps=steps, batch_major=batch_major),
        grid=(seq // steps,),
        in_specs=[pl.BlockSpec(blk, idx),
                  const(nb, S5_STATES), const(nb, S5_STATES), const(2, S5_STATES),
                  const(2, S5_HALF_IN, 2 * S5_HALF_ST), const(2, S5_HALF_ST, S5_HALF_IN),
                  const(2, S5_HALF_ST, S5_HALF_IN), const(1, A_WIDTH), const(A_WIDTH, A_WIDTH)] + extra_specs,
        out_specs=[pl.BlockSpec(blk, idx), const(nb, S5_STATES), const(nb, S5_STATES)],
        out_shape=[jax.ShapeDtypeStruct(out_full, F32), jax.ShapeDtypeStruct((nb, S5_STATES), F32),
                   jax.ShapeDtypeStruct((nb, S5_STATES), F32)],
        scratch_shapes=[pltpu.VMEM((rows, 2 * S5_STATES), F32), pltpu.VMEM((nb, S5_STATES), F32),
                        pltpu.VMEM((nb, S5_STATES), F32)],
        compiler_params=_cparams(("arbitrary",)), name="s5_glu",
    )(u, h0_re, h0_im, a, win, cre, cim, d_skip.reshape(1, A_WIDTH), w_glu_bf16, *extra_args)


GDN_BA_COLS = 128
HI = lax.Precision.HIGHEST


def _gdn_pre_kernel(x_ref, conv0_ref, ba_ref, w_ref, alog_ref, dtb_ref, q_ref, k_ref, v_ref, bg_ref, xp_sc,
                    *, tm, tmo):
    j = pl.program_id(1)

    @pl.when(j == 0)
    def _():
        xp_sc[0:8, :] = conv0_ref[0]

    @pl.when(j > 0)
    def _():
        xp_sc[0:8, :] = xp_sc[tm:tm + 8, :]

    xp_sc[8:8 + tm, :] = x_ref[0]
    conv = w_ref[0:1, :] * xp_sc[pl.ds(5, tm), :]
    for jj in range(1, C_CONV):
        conv = conv + w_ref[jj:jj + 1, :] * xp_sc[pl.ds(5 + jj, tm), :]
    act = conv * jax.nn.sigmoid(conv)
    pad = tmo - tm

    def put(ref, val):
        if pad:
            val = jnp.concatenate([val, jnp.zeros((pad, val.shape[1]), F32)], axis=0)
        ref[0] = val

    qs, ks = [], []
    for h in range(C_HEADS):
        qh = act[:, h * C_DK:(h + 1) * C_DK]
        kh = act[:, C_QK + h * C_DK:C_QK + (h + 1) * C_DK]
        qs.append(qh * lax.rsqrt(jnp.sum(qh * qh, axis=-1, keepdims=True) + EPS) * (C_DK ** -0.5))
        ks.append(kh * lax.rsqrt(jnp.sum(kh * kh, axis=-1, keepdims=True) + EPS))
    put(q_ref, jnp.concatenate(qs, axis=-1))
    put(k_ref, jnp.concatenate(ks, axis=-1))
    put(v_ref, act[:, 2 * C_QK:])
    ba = ba_ref[0]
    lane = lax.broadcasted_iota(jnp.int32, ba.shape, 1)
    beta = jax.nn.sigmoid(ba)
    g = -jnp.exp(alog_ref[...]) * jax.nn.softplus(ba + dtb_ref[...])
    bg = jnp.where(lane < C_HEADS, beta, jnp.where(lane < 2 * C_HEADS, g, 0.0))
    put(bg_ref, bg)


def _gdn_pre(proj3, conv0, conv_w, a_log, dt_bias, tm, tmo):
    bsz, seq, _ = proj3.shape
    nblk = seq // tm
    conv0_p = jnp.pad(conv0, ((0, 0), (8 - (C_CONV - 1), 0), (0, 0)))
    vec = lambda x: jnp.zeros((1, GDN_BA_COLS), F32).at[0, C_HEADS:2 * C_HEADS].set(x)
    ba_blk = (C_CONV_CH + C_V) // GDN_BA_COLS
    out_len = nblk * tmo
    return pl.pallas_call(
        functools.partial(_gdn_pre_kernel, tm=tm, tmo=tmo),
        grid=(bsz, nblk),
        in_specs=[pl.BlockSpec((1, tm, C_CONV_CH), lambda b, j: (b, j, 0)),
                  pl.BlockSpec((1, 8, C_CONV_CH), lambda b, j: (b, 0, 0)),
                  pl.BlockSpec((1, tm, GDN_BA_COLS), lambda b, j: (b, j, ba_blk)),
                  pl.BlockSpec((C_CONV, C_CONV_CH), lambda b, j: (0, 0)),
                  pl.BlockSpec((1, GDN_BA_COLS), lambda b, j: (0, 0)),
                  pl.BlockSpec((1, GDN_BA_COLS), lambda b, j: (0, 0))],
        out_specs=[pl.BlockSpec((1, tmo, C_QK), lambda b, j: (b, j, 0)),
                   pl.BlockSpec((1, tmo, C_QK), lambda b, j: (b, j, 0)),
                   pl.BlockSpec((1, tmo, C_V), lambda b, j: (b, j, 0)),
                   pl.BlockSpec((1, tmo, GDN_BA_COLS), lambda b, j: (b, j, 0))],
        out_shape=[jax.ShapeDtypeStruct((bsz, out_len, C_QK), F32), jax.ShapeDtypeStruct((bsz, out_len, C_QK), F32),
                   jax.ShapeDtypeStruct((bsz, out_len, C_V), F32),
                   jax.ShapeDtypeStruct((bsz, out_len, GDN_BA_COLS), F32)],
        scratch_shapes=[pltpu.VMEM((tm + 16, C_CONV_CH), F32)],
        compiler_params=_cparams(("parallel", "arbitrary")), name="gdn_pre",
    )(proj3, conv0_p, proj3, conv_w, vec(a_log), vec(dt_bias))


def _dot_hi(a, b):
    return jnp.dot(a, b, preferred_element_type=F32, precision=HI)


def _dot_bf(a, b):
    return jnp.dot(a.astype(BF16), b.astype(BF16), preferred_element_type=F32)


_NN = (((1,), (0,)), ((), ()))
_NT = (((1,), (1,)), ((), ()))


def _split2(x):
    hi = x.astype(BF16)
    return hi, (x - hi.astype(F32)).astype(BF16)


def _dot3(a2, b2, dims=_NN):
    d = lambda x, y: lax.dot_general(x, y, dims, preferred_element_type=F32)
    return d(a2[0], b2[0]) + (d(a2[0], b2[1]) + d(a2[1], b2[0]))


def _dot_sel(sel_bf16, b, dims=_NN):
    b1 = b.astype(BF16)
    r1 = b - b1.astype(F32)
    b2 = r1.astype(BF16)
    b3 = (r1 - b2.astype(F32)).astype(BF16)
    d = lambda y: lax.dot_general(sel_bf16, y, dims, preferred_element_type=F32)
    return d(b1) + (d(b2) + d(b3))


def _gdn_chunk_kernel(q_ref, k_ref, v_ref, bg_ref, z_ref, s0_ref, ng_ref, y_ref, s_out, s_sc, *, chunk, nchunk, zlen):
    j = pl.program_id(1)

    @pl.when(j == 0)
    def _():
        s_sc[...] = s0_ref[0]

    ri = lax.broadcasted_iota(jnp.int32, (chunk, chunk), 0)
    ci = lax.broadcasted_iota(jnp.int32, (chunk, chunk), 1)
    causal = ci <= ri
    strict = ci < ri
    ltri = causal.astype(BF16)
    eye = (ci == ri).astype(F32)
    e0 = (lax.broadcasted_iota(jnp.int32, (chunk, C_DK), 1) == 0).astype(BF16)
    n_dbl = chunk.bit_length() - 2

    def do_chunk(c, carry):
        r = pl.ds(pl.multiple_of(c * chunk, chunk), chunk)
        bg = bg_ref[0, r, :]
        for h in range(C_HEADS):
            q = q_ref[0, r, h * C_DK:(h + 1) * C_DK]
            k = k_ref[0, r, h * C_DK:(h + 1) * C_DK]
            v = v_ref[0, r, h * C_DV:(h + 1) * C_DV]
            beta = bg[:, h:h + 1]
            g = bg[:, C_HEADS + h:C_HEADS + h + 1]
            gcol = _dot_sel(ltri, jnp.broadcast_to(g, (chunk, C_DK)))
            grow = _dot_sel(e0, gcol, _NT)
            decay = jnp.where(causal, jnp.exp(jnp.where(causal, gcol[:, :chunk] - grow, 0.0)), 0.0)
            kb = k * beta
            vb = v * beta
            k_2 = _split2(k)
            m = jnp.where(strict, _dot3(_split2(kb), k_2, _NT) * decay, 0.0)
            tinv = eye - m
            mp = m.astype(BF16)
            for _ in range(n_dbl):
                mp = jnp.dot(mp, mp, preferred_element_type=F32).astype(BF16)
                tinv = tinv + jnp.dot(tinv.astype(BF16), mp, preferred_element_type=F32)
            tinv_b = tinv.astype(BF16)
            u = _dot_bf(tinv_b, vb)
            w = _dot_bf(tinv_b, kb * jnp.exp(gcol))
            s = s_sc[h]
            s_b = s.astype(BF16)
            v_new = u - _dot_bf(w, s_b)
            a_intra = jnp.where(causal, _dot3(_split2(q), k_2, _NT), 0.0) * decay
            o = _dot_bf(q * jnp.exp(gcol), s_b) + _dot_bf(a_intra, v_new)
            g_last = gcol[chunk - 1:chunk, :]
            k2 = k * jnp.exp(g_last - gcol)
            s_sc[h] = s * jnp.exp(g_last) + lax.dot_general(k2.astype(BF16), v_new.astype(BF16),
                                                            (((0,), (0,)), ((), ())), preferred_element_type=F32)
            o = o * lax.rsqrt(jnp.mean(o * o, axis=-1, keepdims=True) + EPS) * ng_ref[...]
            z = z_ref[0, pl.ds(pl.multiple_of(c * chunk, chunk), zlen), h * C_DV:(h + 1) * C_DV]
            y_ref[0, pl.ds(pl.multiple_of(c * chunk, chunk), zlen), h * C_DV:(h + 1) * C_DV] = (
                o[:zlen] * (z * jax.nn.sigmoid(z)))
        return carry

    lax.fori_loop(0, nchunk, do_chunk, 0)

    @pl.when(j == pl.num_programs(1) - 1)
    def _():
        s_out[0] = s_sc[...]


def _gdn_chunks(q, k, v, bg, proj3, s0, norm_g, chunk, nchunk, zlen):
    bsz, out_len, _ = q.shape
    rb = chunk * nchunk
    nblk = out_len // rb
    zrows = zlen if zlen < chunk else rb
    z_blk = C_CONV_CH // C_V
    row = lambda w: pl.BlockSpec((1, rb, w), lambda b, j: (b, j, 0))
    return pl.pallas_call(
        functools.partial(_gdn_chunk_kernel, chunk=chunk, nchunk=nchunk, zlen=min(zlen, chunk)),
        grid=(bsz, nblk),
        in_specs=[row(C_QK), row(C_QK), row(C_V), row(GDN_BA_COLS),
                  pl.BlockSpec((1, zrows, C_V), lambda b, j: (b, j, z_blk)),
                  pl.BlockSpec((1, C_HEADS, C_DK, C_DV), lambda b, j: (b, 0, 0, 0)),
                  pl.BlockSpec((1, C_DV), lambda b, j: (0, 0))],
        out_specs=[pl.BlockSpec((1, zrows, C_V), lambda b, j: (b, j, 0)),
                   pl.BlockSpec((1, C_HEADS, C_DK, C_DV), lambda b, j: (b, 0, 0, 0))],
        out_shape=[jax.ShapeDtypeStruct((bsz, nblk * zrows, C_V), F32),
                   jax.ShapeDtypeStruct((bsz, C_HEADS, C_DK, C_DV), F32)],
        scratch_shapes=[pltpu.VMEM((C_HEADS, C_DK, C_DV), F32)],
        compiler_params=_cparams(("parallel", "arbitrary")), name="gdn_chunks",
    )(q, k, v, bg, proj3, s0, norm_g.reshape(1, C_DV))


def _tri_pairs(n):
    qi = [i for i in range(n) for _ in range(i + 1)]
    kj = [j for i in range(n) for j in range(i + 1)]
    return jnp.asarray(qi, jnp.int32), jnp.asarray(kj, jnp.int32)


def _softmax_update(s, v_bf16, m_ref, l_ref, acc_ref, idx):
    m_prev = m_ref[idx]
    m_new = jnp.maximum(m_prev, jnp.max(s, axis=-1, keepdims=True))
    corr = jnp.exp(m_prev - m_new)
    p = jnp.exp(s - m_new)
    l_ref[idx] = corr * l_ref[idx] + jnp.sum(p, axis=-1, keepdims=True)
    acc_ref[idx] = corr * acc_ref[idx] + jnp.dot(p.astype(BF16), v_bf16, preferred_element_type=F32)
    m_ref[idx] = m_new


def _diff_attn_kernel(qi_ref, kj_ref, lam_ref, q_ref, k_ref, v_ref, g_ref, o_ref, m_ref, l_ref, acc_ref,
                      *, blk, out_scale):
    p = pl.program_id(2)
    qi = qi_ref[p]
    kj = kj_ref[p]

    @pl.when(kj == 0)
    def _():
        m_ref[...] = jnp.full(m_ref.shape, NEG_BIG, F32)
        l_ref[...] = jnp.zeros(l_ref.shape, F32)
        acc_ref[...] = jnp.zeros(acc_ref.shape, F32)

    def step(masked):
        q = (q_ref[...] * (B_DK ** -0.5)).astype(BF16)
        k = k_ref[...].astype(BF16)
        v = v_ref[...].astype(BF16)
        for c in range(2):
            s = lax.dot_general(q[:, c * B_DK:(c + 1) * B_DK], k[:, c * B_DK:(c + 1) * B_DK],
                                (((1,), (1,)), ((), ())), preferred_element_type=F32)
            if masked:
                row = lax.broadcasted_iota(jnp.int32, s.shape, 0)
                col = lax.broadcasted_iota(jnp.int32, s.shape, 1)
                s = jnp.where(col <= row, s, NEG_BIG)
            _softmax_update(s, v, m_ref, l_ref, acc_ref, c)

    @pl.when(kj < qi)
    def _():
        step(False)

    @pl.when(kj == qi)
    def _():
        step(True)
        lam = lam_ref[0, 0]
        o0 = acc_ref[0] / l_ref[0]
        o1 = acc_ref[1] / l_ref[1]
        y = o0 - lam * o1
        y = y * lax.rsqrt(jnp.mean(y * y, axis=-1, keepdims=True) + EPS)
        o_ref[...] = y * g_ref[...] * out_scale


def _diff_attn_prompt(q, k, v, lam, subln, bsz, seq, lam_init, blk=512):
    nq = seq // blk
    qi, kj = _tri_pairs(nq)
    kern = functools.partial(_diff_attn_kernel, blk=blk, out_scale=1.0 - lam_init)
    grid_spec = pltpu.PrefetchScalarGridSpec(
        num_scalar_prefetch=2,
        grid=(bsz, B_HEADS, qi.shape[0]),
        in_specs=[
            pl.BlockSpec(memory_space=pltpu.SMEM),
            pl.BlockSpec((blk, B_DV), lambda b, h, p, qi, kj: (b * nq + qi[p], h)),
            pl.BlockSpec((blk, B_DV), lambda b, h, p, qi, kj: (b * nq + kj[p], h)),
            pl.BlockSpec((blk, B_DV), lambda b, h, p, qi, kj: (b * nq + kj[p], h)),
            pl.BlockSpec((1, B_DV), lambda b, h, p, qi, kj: (0, 0)),
        ],
        out_specs=pl.BlockSpec((blk, B_DV), lambda b, h, p, qi, kj: (b * nq + qi[p], h)),
        scratch_shapes=[pltpu.VMEM((2, blk, 1), F32), pltpu.VMEM((2, blk, 1), F32),
                        pltpu.VMEM((2, blk, B_DV), F32)],
    )
    return pl.pallas_call(
        kern, grid_spec=grid_spec, out_shape=jax.ShapeDtypeStruct(q.shape, F32),
        compiler_params=_cparams(("parallel", "parallel", "arbitrary")), name="diff_attn_prompt",
    )(qi, kj, lam.reshape(1, 1), q, k, v, subln.reshape(1, B_DV))


def _mla_attn_kernel(qi_ref, kj_ref, ql_ref, qr_ref, c_ref, kr_ref, wuv_ref, o_ref, m_ref, l_ref, acc_ref, *, blk):
    p = pl.program_id(1)
    qi = qi_ref[p]
    kj = kj_ref[p]
    scale = (D_NOPE + D_ROPE) ** -0.5

    @pl.when(kj == 0)
    def _():
        m_ref[...] = jnp.full(m_ref.shape, NEG_BIG, F32)
        l_ref[...] = jnp.zeros(l_ref.shape, F32)
        acc_ref[...] = jnp.zeros(acc_ref.shape, F32)

    def step(masked):
        c = c_ref[...].astype(BF16)
        kr = kr_ref[...].astype(BF16)
        for h in range(D_HEADS):
            ql = (ql_ref[:, h * D_KV_LORA:(h + 1) * D_KV_LORA] * scale).astype(BF16)
            qr = (qr_ref[:, h * D_ROPE:(h + 1) * D_ROPE] * scale).astype(BF16)
            s = (lax.dot_general(ql, c, (((1,), (1,)), ((), ())), preferred_element_type=F32)
                 + lax.dot_general(qr, kr, (((1,), (1,)), ((), ())), preferred_element_type=F32))
            if masked:
                row = lax.broadcasted_iota(jnp.int32, s.shape, 0)
                col = lax.broadcasted_iota(jnp.int32, s.shape, 1)
                s = jnp.where(col <= row, s, NEG_BIG)
            _softmax_update(s, c, m_ref, l_ref, acc_ref, h)

    @pl.when(kj < qi)
    def _():
        step(False)

    @pl.when(kj == qi)
    def _():
        step(True)
        for h in range(D_HEADS):
            o = (acc_ref[h] / l_ref[h]).astype(BF16)
            o_ref[:, h * D_DV:(h + 1) * D_DV] = jnp.dot(o, wuv_ref[h], preferred_element_type=F32)


def _mla_attn_prompt(q_lat, q_rope, c, kr, w_uv_bf16, bsz, seq, blk=512):
    nq = seq // blk
    qi, kj = _tri_pairs(nq)
    grid_spec = pltpu.PrefetchScalarGridSpec(
        num_scalar_prefetch=2,
        grid=(bsz, qi.shape[0]),
        in_specs=[
            pl.BlockSpec((blk, D_HEADS * D_KV_LORA), lambda b, p, qi, kj: (b * nq + qi[p], 0)),
            pl.BlockSpec((blk, D_HEADS * D_ROPE), lambda b, p, qi, kj: (b * nq + qi[p], 0)),
            pl.BlockSpec((blk, D_KV_LORA), lambda b, p, qi, kj: (b * nq + kj[p], 0)),
            pl.BlockSpec((blk, D_ROPE), lambda b, p, qi, kj: (b * nq + kj[p], 0)),
            pl.BlockSpec((D_HEADS, D_KV_LORA, D_DV), lambda b, p, qi, kj: (0, 0, 0)),
        ],
        out_specs=pl.BlockSpec((blk, D_HEADS * D_DV), lambda b, p, qi, kj: (b * nq + qi[p], 0)),
        scratch_shapes=[pltpu.VMEM((D_HEADS, blk, 1), F32), pltpu.VMEM((D_HEADS, blk, 1), F32),
                        pltpu.VMEM((D_HEADS, blk, D_KV_LORA), F32)],
    )
    return pl.pallas_call(
        functools.partial(_mla_attn_kernel, blk=blk), grid_spec=grid_spec,
        out_shape=jax.ShapeDtypeStruct((q_lat.shape[0], D_HEADS * D_DV), F32),
        compiler_params=_cparams(("parallel", "arbitrary")), name="mla_attn_prompt",
    )(qi, kj, q_lat, q_rope, c, kr, w_uv_bf16)


PAGES_PER_STEP = 16
NEW_PAD = 8


def _page_spec(width, i, n_pages, base):
    return pl.BlockSpec((1, PAGE_SIZE, width),
                        lambda b, p, pt: (pt[b * n_pages + p * PAGES_PER_STEP + i] + base, 0, 0))


def _flash_update(s, v_bf16, m_ref, l_ref, acc_ref):
    m_prev = m_ref[...]
    m_new = jnp.maximum(m_prev, jnp.max(s, axis=-1, keepdims=True))
    corr = jnp.exp(m_prev - m_new)
    p = jnp.exp(s - m_new)
    l_ref[...] = corr * l_ref[...] + jnp.sum(p, axis=-1, keepdims=True)
    acc_ref[...] = corr * acc_ref[...] + jnp.dot(p.astype(BF16), v_bf16, preferred_element_type=F32)
    m_ref[...] = m_new


def _flash_update_pages(s, values, m_ref, l_ref, acc_ref):
    m_prev = m_ref[...]
    m_new = jnp.maximum(m_prev, jnp.max(s, axis=-1, keepdims=True))
    corr = jnp.exp(m_prev - m_new)
    p = jnp.exp(s - m_new)
    l_ref[...] = corr * l_ref[...] + jnp.sum(p, axis=-1, keepdims=True)
    pb = p.astype(BF16)
    pv = jnp.dot(pb[:, :PAGE_SIZE], values[0].astype(BF16), preferred_element_type=F32)
    for i in range(1, len(values)):
        pv = pv + jnp.dot(pb[:, i * PAGE_SIZE:(i + 1) * PAGE_SIZE], values[i].astype(BF16),
                          preferred_element_type=F32)
    acc_ref[...] = corr * acc_ref[...] + pv
    m_ref[...] = m_new


def _self_mask(s, t_new):
    row_t = lax.broadcasted_iota(jnp.int32, s.shape, 0) % t_new
    col = lax.broadcasted_iota(jnp.int32, s.shape, 1)
    return jnp.where(col <= row_t, s, NEG_BIG)


def _diff_paged_kernel(pt_ref, lam_ref, q_ref, kn_ref, vn_ref, g_ref, *refs, out_scale):
    k_refs = refs[:PAGES_PER_STEP]
    v_refs = refs[PAGES_PER_STEP:2 * PAGES_PER_STEP]
    o_ref, m_ref, l_ref, acc_ref = refs[2 * PAGES_PER_STEP:]
    p = pl.program_id(1)

    @pl.when(p == 0)
    def _():
        m_ref[...] = jnp.full(m_ref.shape, NEG_BIG, F32)
        l_ref[...] = jnp.zeros(l_ref.shape, F32)
        acc_ref[...] = jnp.zeros(acc_ref.shape, F32)

    q = (q_ref[0] * (B_DK ** -0.5)).astype(BF16)
    hrows = 2 * NEW_PAD
    for h in range(B_HEADS):
        rs = pl.ds(h * hrows, hrows)
        qh = q[h * hrows:(h + 1) * hrows]
        s = jnp.concatenate([lax.dot_general(qh, k_refs[i][0, 0, :, h, :].astype(BF16), _NT,
                                             preferred_element_type=F32) for i in range(PAGES_PER_STEP)], axis=-1)
        _flash_update_pages(s, [r[0, 0, :, h, :] for r in v_refs], m_ref.at[rs], l_ref.at[rs], acc_ref.at[rs])

    @pl.when(p == pl.num_programs(1) - 1)
    def _():
        for h in range(B_HEADS):
            rs = pl.ds(h * hrows, hrows)
            qh = q[h * hrows:(h + 1) * hrows]
            kn = kn_ref[0, :, h * B_DV:(h + 1) * B_DV].astype(BF16)
            s = lax.dot_general(qh, kn, _NT, preferred_element_type=F32)
            _flash_update(_self_mask(s, NEW_PAD), vn_ref[0, :, h * B_DV:(h + 1) * B_DV].astype(BF16),
                          m_ref.at[rs], l_ref.at[rs], acc_ref.at[rs])
            o = acc_ref[rs, :] / l_ref[rs, :]
            y = o[:NEW_PAD] - lam_ref[0, 0] * o[NEW_PAD:]
            o_ref[0, :, h * B_DV:(h + 1) * B_DV] = (
                y * lax.rsqrt(jnp.mean(y * y, axis=-1, keepdims=True) + EPS) * g_ref[...] * out_scale)


def _diff_attn_paged(q, k, v, cache_k, cache_v, layer, page_table, lam, subln, lam_init):
    bd, t_new = q.shape[:2]
    n_pages = page_table.shape[1]
    qe = jnp.einsum('bthcd,ce->bhcted', q, jnp.eye(2, dtype=F32))
    qe = jnp.pad(qe, ((0, 0), (0, 0), (0, 0), (0, NEW_PAD - t_new), (0, 0), (0, 0)))
    rows = B_HEADS * 2 * NEW_PAD
    qe = qe.reshape(bd, rows, 2 * B_DK)
    padn = lambda x: jnp.pad(x.reshape(bd, t_new, -1), ((0, 0), (0, NEW_PAD - t_new), (0, 0)))
    width = B_HEADS * B_DV
    per_b = lambda r, w: pl.BlockSpec((1, r, w), lambda b, p, pt: (b, 0, 0))
    page = lambda i, w: pl.BlockSpec(
        (1, 1, PAGE_SIZE, B_HEADS, w),
        lambda b, p, pt: (layer, pt[b * n_pages + p * PAGES_PER_STEP + i], 0, 0, 0))
    grid_spec = pltpu.PrefetchScalarGridSpec(
        num_scalar_prefetch=1,
        grid=(bd, n_pages // PAGES_PER_STEP),
        in_specs=[pl.BlockSpec(memory_space=pltpu.SMEM), per_b(rows, 2 * B_DK), per_b(NEW_PAD, width),
                  per_b(NEW_PAD, width), pl.BlockSpec((1, B_DV), lambda b, p, pt: (0, 0))]
        + [page(i, 2 * B_DK) for i in range(PAGES_PER_STEP)] + [page(i, B_DV) for i in range(PAGES_PER_STEP)],
        out_specs=per_b(NEW_PAD, width),
        scratch_shapes=[pltpu.VMEM((rows, 1), F32), pltpu.VMEM((rows, 1), F32), pltpu.VMEM((rows, B_DV), F32)],
    )
    y = pl.pallas_call(
        functools.partial(_diff_paged_kernel, out_scale=1.0 - lam_init), grid_spec=grid_spec,
        out_shape=jax.ShapeDtypeStruct((bd, NEW_PAD, width), F32),
        compiler_params=_cparams(("parallel", "arbitrary")), name="diff_attn_paged",
    )(page_table.reshape(-1), lam.reshape(1, 1), qe, padn(k), padn(v), subln.reshape(1, B_DV),
      *([cache_k] * PAGES_PER_STEP), *([cache_v] * PAGES_PER_STEP))
    return y[:, :t_new]


def _mla_paged_kernel(pt_ref, ql_ref, qr_ref, cn_ref, rn_ref, *refs, t_new):
    c_refs = refs[:PAGES_PER_STEP]
    r_refs = refs[PAGES_PER_STEP:2 * PAGES_PER_STEP]
    o_ref, m_ref, l_ref, acc_ref = refs[2 * PAGES_PER_STEP:]
    p = pl.program_id(1)
    scale = (D_NOPE + D_ROPE) ** -0.5

    @pl.when(p == 0)
    def _():
        m_ref[...] = jnp.full(m_ref.shape, NEG_BIG, F32)
        l_ref[...] = jnp.zeros(l_ref.shape, F32)
        acc_ref[...] = jnp.zeros(acc_ref.shape, F32)

    ql = (ql_ref[0] * scale).astype(BF16)
    qr = (qr_ref[0] * scale).astype(BF16)
    nt = (((1,), (1,)), ((), ()))

    def scores(c_bf16, r_bf16):
        return (lax.dot_general(ql, c_bf16, nt, preferred_element_type=F32)
                + lax.dot_general(qr, r_bf16, nt, preferred_element_type=F32))

    cs = [c_refs[i][0].astype(BF16) for i in range(PAGES_PER_STEP)]
    s = jnp.concatenate([scores(cs[i], r_refs[i][0].astype(BF16)) for i in range(PAGES_PER_STEP)], axis=-1)
    _flash_update_pages(s, cs, m_ref, l_ref, acc_ref)

    @pl.when(p == pl.num_programs(1) - 1)
    def _():
        c = cn_ref[0].astype(BF16)
        _flash_update(_self_mask(scores(c, rn_ref[0].astype(BF16)), t_new), c, m_ref, l_ref, acc_ref)
        o_ref[0] = acc_ref[...] / l_ref[...]


def _mla_attn_paged(q_lat, q_rope, c, kr, cache_lat, cache_rope, layer, page_table):
    bd, t_new = q_lat.shape[:2]
    n_pool = cache_lat.shape[1]
    n_pages = page_table.shape[1]
    cl = cache_lat.reshape(-1, PAGE_SIZE, D_KV_LORA)
    cr = cache_rope.reshape(-1, PAGE_SIZE, D_ROPE)
    rows = D_HEADS * t_new
    ql = q_lat.transpose(0, 2, 1, 3).reshape(bd, rows, D_KV_LORA)
    qr = q_rope.transpose(0, 2, 1, 3).reshape(bd, rows, D_ROPE)
    padn = lambda x: jnp.pad(x, ((0, 0), (0, NEW_PAD - t_new), (0, 0)))
    per_b = lambda r, w: pl.BlockSpec((1, r, w), lambda b, p, pt: (b, 0, 0))
    grid_spec = pltpu.PrefetchScalarGridSpec(
        num_scalar_prefetch=1,
        grid=(bd, n_pages // PAGES_PER_STEP),
        in_specs=[per_b(rows, D_KV_LORA), per_b(rows, D_ROPE), per_b(NEW_PAD, D_KV_LORA), per_b(NEW_PAD, D_ROPE)]
        + [_page_spec(D_KV_LORA, i, n_pages, layer * n_pool) for i in range(PAGES_PER_STEP)]
        + [_page_spec(D_ROPE, i, n_pages, layer * n_pool) for i in range(PAGES_PER_STEP)],
        out_specs=per_b(rows, D_KV_LORA),
        scratch_shapes=[pltpu.VMEM((rows, 1), F32), pltpu.VMEM((rows, 1), F32), pltpu.VMEM((rows, D_KV_LORA), F32)],
    )
    o = pl.pallas_call(
        functools.partial(_mla_paged_kernel, t_new=t_new), grid_spec=grid_spec,
        out_shape=jax.ShapeDtypeStruct((bd, rows, D_KV_LORA), F32),
        compiler_params=_cparams(("parallel", "arbitrary")), name="mla_attn_paged",
    )(page_table.reshape(-1), ql, qr, padn(c), padn(kr), *([cl] * PAGES_PER_STEP), *([cr] * PAGES_PER_STEP))
    return o.reshape(bd, D_HEADS, t_new, D_KV_LORA).transpose(0, 2, 1, 3)


ROUTER_PAD = 128


def _out_ln_router_kernel(xa_ref, xb_ref, wa_ref, wb_ref, h_ref, g_ref, b_ref, rw_ref, rb_ref, o_ref, lg_ref):
    mix = (jnp.dot(xa_ref[...].astype(BF16), wa_ref[...], preferred_element_type=F32)
           + jnp.dot(xb_ref[...].astype(BF16), wb_ref[...], preferred_element_type=F32))
    x = DN_ALPHA * h_ref[...] + mix
    mu = jnp.mean(x, axis=-1, keepdims=True)
    xc = x - mu
    var = jnp.mean(xc * xc, axis=-1, keepdims=True)
    y = xc * lax.rsqrt(var + EPS) * g_ref[...] + b_ref[...]
    o_ref[...] = y
    lg_ref[...] = _dot_hi(y, rw_ref[...]) + rb_ref[...]


def _out_ln_router(xa, xb, w_out, h, ln_g, ln_b, router_w, router_b, tm=256):
    n, d = h.shape
    ka, kb = xa.shape[1], xb.shape[1]
    tm = min(tm, n)
    rw = jnp.pad(router_w, ((0, 0), (0, ROUTER_PAD - N_EXPERTS)))
    rb = jnp.pad(router_b, (0, ROUTER_PAD - N_EXPERTS)).reshape(1, ROUTER_PAD)
    row = lambda w: pl.BlockSpec((tm, w), lambda i: (i, 0))
    const = lambda r, w: pl.BlockSpec((r, w), lambda i: (0, 0))
    return pl.pallas_call(
        _out_ln_router_kernel, grid=(n // tm,),
        in_specs=[row(ka), row(kb), const(ka, d), const(kb, d), row(d), const(1, d), const(1, d),
                  const(d, ROUTER_PAD), const(1, ROUTER_PAD)],
        out_specs=[row(d), row(ROUTER_PAD)],
        out_shape=[jax.ShapeDtypeStruct((n, d), F32), jax.ShapeDtypeStruct((n, ROUTER_PAD), F32)],
        compiler_params=_cparams(("parallel",)), name="out_ln_router",
    )(xa, xb, w_out[:ka].astype(BF16), w_out[ka:].astype(BF16), h, ln_g.reshape(1, d), ln_b.reshape(1, d), rw, rb)


def _combine_ln_kernel(*refs):
    ya_refs = refs[:TOP_K]
    gt_ref, h_ref, g_ref, b_ref, o_ref = refs[TOP_K:]
    gates = gt_ref[...]
    f = gates[:, 0:1] * ya_refs[0][...]
    for kk in range(1, TOP_K):
        f = f + gates[:, kk:kk + 1] * ya_refs[kk][...]
    x = DN_ALPHA * h_ref[...] + f
    mu = jnp.mean(x, axis=-1, keepdims=True)
    xc = x - mu
    var = jnp.mean(xc * xc, axis=-1, keepdims=True)
    o_ref[...] = xc * lax.rsqrt(var + EPS) * g_ref[...] + b_ref[...]


def _combine_ln(y_assign, gates, h, ln_g, ln_b, tm=256):
    n, d = h.shape
    tm = min(tm, n)
    nblk = n // tm
    row = lambda w: pl.BlockSpec((tm, w), lambda i: (i, 0))
    const = lambda r, w: pl.BlockSpec((r, w), lambda i: (0, 0))
    choice = lambda kk: pl.BlockSpec((tm, d), lambda i: (kk * nblk + i, 0))
    return pl.pallas_call(
        _combine_ln_kernel, grid=(nblk,),
        in_specs=[choice(kk) for kk in range(TOP_K)] + [row(TOP_K), row(d), const(1, d), const(1, d)],
        out_specs=row(d), out_shape=jax.ShapeDtypeStruct((n, d), F32),
        compiler_params=_cparams(("parallel",)), name="combine_ln",
    )(*([y_assign] * TOP_K), gates, h, ln_g.reshape(1, d), ln_b.reshape(1, d))


def _moe_kernel(be_ref, nu_ref, x_ref, wgu_ref, bgu_ref, wd_ref, bd_ref, o_ref, wgu_bf, wd_bf):
    i = pl.program_id(0)
    e = be_ref[i]
    e_prev = be_ref[jnp.maximum(i - 1, 0)]

    @pl.when((i == 0) | (e != e_prev))
    def _():
        wgu_bf[...] = wgu_ref[0].astype(BF16)
        wd_bf[...] = wd_ref[0].astype(BF16)

    @pl.when(i < nu_ref[0])
    def _():
        h = jnp.dot(x_ref[...].astype(BF16), wgu_bf[...], preferred_element_type=F32) + bgu_ref[0]
        gate = jnp.minimum(h[:, :D_FF], SWIGLU_LIMIT)
        up = jnp.clip(h[:, D_FF:], -SWIGLU_LIMIT, SWIGLU_LIMIT)
        act = (up + 1.0) * gate * jax.nn.sigmoid(SWIGLU_ALPHA * gate)
        o_ref[...] = jnp.dot(act.astype(BF16), wd_bf[...], preferred_element_type=F32) + bd_ref[0]

    @pl.when(i >= nu_ref[0])
    def _():
        o_ref[...] = jnp.zeros(o_ref.shape, F32)


def _moe_experts(x_sorted, block_e, n_used, w_gu, b_gu, w_down, b_down, blk):
    p_rows, d = x_sorted.shape
    n_blocks = p_rows // blk
    grid_spec = pltpu.PrefetchScalarGridSpec(
        num_scalar_prefetch=2,
        grid=(n_blocks,),
        in_specs=[
            pl.BlockSpec((blk, d), lambda i, be, nu: (i, 0)),
            pl.BlockSpec((1, d, 2 * D_FF), lambda i, be, nu: (be[i], 0, 0)),
            pl.BlockSpec((1, 1, 2 * D_FF), lambda i, be, nu: (be[i], 0, 0)),
            pl.BlockSpec((1, D_FF, d), lambda i, be, nu: (be[i], 0, 0)),
            pl.BlockSpec((1, 1, d), lambda i, be, nu: (be[i], 0, 0)),
        ],
        out_specs=pl.BlockSpec((blk, d), lambda i, be, nu: (i, 0)),
        scratch_shapes=[pltpu.VMEM((d, 2 * D_FF), BF16), pltpu.VMEM((D_FF, d), BF16)],
    )
    return pl.pallas_call(
        _moe_kernel, grid_spec=grid_spec, out_shape=jax.ShapeDtypeStruct((p_rows, d), F32),
        compiler_params=_cparams(("arbitrary",)), name="moe_experts",
    )(block_e, n_used, x_sorted, w_gu, b_gu.reshape(N_EXPERTS, 1, 2 * D_FF), w_down,
      b_down.reshape(N_EXPERTS, 1, d))


def _moe(xt, logits, w_gu, b_gu, w_down, b_down, ln_g, ln_b):
    n, d = xt.shape
    blk = 256 if n >= 8192 else 128
    top_val, top_idx = lax.top_k(logits, TOP_K)
    gates = jax.nn.softmax(top_val, axis=-1)
    nk = n * TOP_K
    flat_e = top_idx.reshape(-1).astype(jnp.int32)
    onehot = (flat_e[:, None] == jnp.arange(N_EXPERTS, dtype=jnp.int32)[None, :]).astype(jnp.int32)
    csum = jnp.cumsum(onehot, axis=0)
    counts = csum[-1]
    rank = jnp.sum((csum - onehot) * onehot, axis=1)
    padded = (counts + blk - 1) // blk * blk
    pend = jnp.cumsum(padded)
    pstart = pend - padded
    dest = pstart[flat_e] + rank
    n_blocks = -(-(nk + N_EXPERTS * (blk - 1)) // blk)
    p_rows = n_blocks * blk
    buf_tok = jnp.zeros((p_rows,), jnp.int32).at[dest].set(jnp.arange(nk, dtype=jnp.int32) // TOP_K,
                                                            unique_indices=True)
    block_e = jnp.minimum(jnp.searchsorted(pend, jnp.arange(n_blocks, dtype=jnp.int32) * blk, side='right'),
                          N_EXPERTS - 1).astype(jnp.int32)
    n_used = (pend[-1] // blk).astype(jnp.int32).reshape(1)
    x_sorted = xt[buf_tok]
    y_buf = _moe_experts(x_sorted, block_e, n_used, w_gu, b_gu, w_down, b_down, blk)
    y_assign = y_buf[dest.reshape(n, TOP_K).T.reshape(-1)]
    return _combine_ln(y_assign, gates, xt, ln_g, ln_b)


def _rms_norm(x, g):
    return x * lax.rsqrt(jnp.mean(x * x, axis=-1, keepdims=True) + EPS) * g


def _rope(x, pos, theta, n_rot):
    half = n_rot // 2
    inv = jnp.exp(-math.log(theta) * jnp.arange(half, dtype=F32) * (2.0 / n_rot))
    ang = pos[:, None] * inv[None, :]
    cos = jnp.cos(ang)[:, None, :]
    sin = jnp.sin(ang)[:, None, :]
    x1 = x[..., :half]
    x2 = x[..., half:n_rot]
    return jnp.concatenate([x1 * cos - x2 * sin, x2 * cos + x1 * sin, x[..., n_rot:]], axis=-1)


def _even_mixer(x, pos, h0_re, h0_im, sample_ctx, lam_init, w_in, lam_re, lam_im, log_dt, b_re, b_im,
                c_re, c_im, d_skip, w_glu, lq1, lk1, lq2, lk2, subln, w_out):
    bsz, L, _ = x.shape
    n = bsz * L
    proj = _mm(x.reshape(n, D_MODEL), w_in.astype(BF16))
    batch_major = L % 8 == 0
    q = proj[:, A_WIDTH:A_WIDTH + B_QK]
    k = proj[:, A_WIDTH + B_QK:A_WIDTH + 2 * B_QK]
    v = proj[:, A_WIDTH + 2 * B_QK:]
    s5w = _s5_weights(lam_re, lam_im, log_dt, b_re, b_im, c_re, c_im)
    proj3 = proj.reshape(bsz, L, proj.shape[1])
    u_in = proj3 if batch_major else proj3[:, :, :A_WIDTH].transpose(1, 0, 2)
    y_s5, h_re, h_im = _s5_glu(u_in, h0_re.reshape(bsz, S5_STATES), h0_im.reshape(bsz, S5_STATES), s5w,
                               d_skip, w_glu.astype(BF16), steps=min(L, 16), batch_major=batch_major)
    y_a = (y_s5 if batch_major else y_s5.transpose(1, 0, 2)).reshape(n, A_WIDTH)
    h_re = h_re.reshape(bsz, A_GROUPS, A_STATE)
    h_im = h_im.reshape(bsz, A_GROUPS, A_STATE)
    q = _rope(q.reshape(bsz, L, 2 * B_HEADS, B_DK), pos, ROPE_THETA, B_ROT)
    k = _rope(k.reshape(bsz, L, 2 * B_HEADS, B_DK), pos, ROPE_THETA, B_ROT)
    lam = jnp.exp(jnp.sum(lq1 * lk1)) - jnp.exp(jnp.sum(lq2 * lk2)) + lam_init
    if sample_ctx is None:
        y_b = _diff_attn_prompt(q.reshape(n, B_QK), k.reshape(n, B_QK), v, lam, subln, bsz, L, lam_init)
    else:
        cache_k, cache_v, layer, page_table = sample_ctx
        y_b = _diff_attn_paged(q.reshape(bsz, L, B_HEADS, 2, B_DK), k.reshape(bsz, L, B_HEADS, 2, B_DK),
                               v.reshape(bsz, L, B_HEADS, B_DV), cache_k, cache_v, layer, page_table, lam, subln,
                               lam_init).reshape(n, B_HEADS * B_DV)
    return ((y_a, y_b, w_out), h_re, h_im, k.reshape(bsz, L, B_HEADS, 2 * B_DK),
            v.reshape(bsz, L, B_HEADS, B_DV))


def _odd_mixer(x, pos, conv0, s0, sample_ctx, w_in, conv_w, a_log, dt_bias, gdn_g,
               q_norm, w_uq, kv_norm, w_uk, w_uv, w_out):
    bsz, L, _ = x.shape
    n = bsz * L
    zcols = lambda w: jnp.zeros((D_MODEL, w), F32)
    ba_end = C_CONV_CH + C_V + 2 * C_HEADS
    w_in_p = jnp.concatenate([w_in[:, :ba_end], zcols(GDN_BA_COLS - 2 * C_HEADS), w_in[:, ba_end:],
                              zcols(128 - D_ROPE)], axis=1).astype(BF16)
    o_cq = C_CONV_CH + C_V + GDN_BA_COLS
    o_ckv = o_cq + D_Q_LORA
    o_kr = o_ckv + D_KV_LORA
    proj = _mm(x.reshape(n, D_MODEL), w_in_p)
    proj3 = proj.reshape(bsz, L, proj.shape[1])
    if L >= C_CHUNK:
        tm, tmo, chunk, nchunk, zlen = 512, 512, C_CHUNK, 512 // C_CHUNK, C_CHUNK
    else:
        tm, tmo, chunk, nchunk, zlen = L, 8, 8, 1, L
    gq, gk, gv, gbg = _gdn_pre(proj3, conv0, conv_w, a_log, dt_bias, tm, tmo)
    y_c, S = _gdn_chunks(gq, gk, gv, gbg, proj3, s0, gdn_g, chunk, nchunk, zlen)
    conv_new = proj3[:, L - (C_CONV - 1):, :C_CONV_CH]
    c_q = proj3[..., o_cq:o_ckv]
    c_kv = proj3[..., o_ckv:o_kr]
    k_r = proj3[..., o_kr:o_kr + D_ROPE]
    qf = _mm(_rms_norm(c_q, q_norm).reshape(n, D_Q_LORA), w_uq.astype(BF16))
    qf = qf.reshape(bsz, L, D_HEADS, D_NOPE + D_ROPE)
    q_rope = _rope(qf[..., D_NOPE:], pos, MLA_ROPE_THETA, D_ROPE)
    q_nope = qf[..., :D_NOPE].reshape(n, D_HEADS * D_NOPE)
    w_bd = jnp.zeros((D_HEADS * D_NOPE, D_HEADS * D_KV_LORA), F32)
    for h in range(D_HEADS):
        w_bd = w_bd.at[h * D_NOPE:(h + 1) * D_NOPE, h * D_KV_LORA:(h + 1) * D_KV_LORA].set(w_uk[h])
    q_lat = _mm(q_nope, w_bd.astype(BF16))
    c = _rms_norm(c_kv, kv_norm)
    kr = _rope(k_r[:, :, None, :], pos, MLA_ROPE_THETA, D_ROPE)[:, :, 0, :]
    if sample_ctx is None:
        y_d = _mla_attn_prompt(q_lat, q_rope.reshape(n, D_HEADS * D_ROPE), c.reshape(n, D_KV_LORA),
                               kr.reshape(n, D_ROPE), w_uv.astype(BF16), bsz, L)
    else:
        cache_lat, cache_rope, layer, page_table = sample_ctx
        o_lat = _mla_attn_paged(q_lat.reshape(bsz, L, D_HEADS, D_KV_LORA), q_rope, c, kr, cache_lat, cache_rope,
                                layer, page_table)
        w_uv_bd = jnp.zeros((D_HEADS * D_KV_LORA, D_HEADS * D_DV), F32)
        for h in range(D_HEADS):
            w_uv_bd = w_uv_bd.at[h * D_KV_LORA:(h + 1) * D_KV_LORA, h * D_DV:(h + 1) * D_DV].set(w_uv[h])
        y_d = _mm(o_lat.reshape(n, D_HEADS * D_KV_LORA), w_uv_bd.astype(BF16))
    return (y_c.reshape(n, C_V), y_d, w_out), S, conv_new, c, kr


def _post_block(h, mix, li, router_w, router_b, moe_w_gu, moe_b_gu, moe_w_down, moe_b_down, ln_g, ln_b):
    bsz, L, d = h.shape
    xa, xb, w_out = mix
    h1, logits = _out_ln_router(xa, xb, w_out, h.reshape(-1, d), ln_g[li, 0], ln_b[li, 0], router_w[li],
                                router_b[li])
    h2 = _moe(h1, logits[:, :N_EXPERTS], moe_w_gu[li], moe_b_gu[li], moe_w_down[li], moe_b_down[li],
              ln_g[li, 1], ln_b[li, 1])
    return h2.reshape(bsz, L, d)


def kernel(x_prompt, x_sample, state_a_re, state_a_im, cache_b_k, cache_b_v, state_c, state_c_conv,
           cache_d_latent, cache_d_rope, page_table, w_in_even, s5_lam_re, s5_lam_im, s5_log_dt,
           s5_b_re, s5_b_im, s5_c_re, s5_c_im, s5_d, s5_w_glu, diff_lq1, diff_lk1, diff_lq2, diff_lk2,
           diff_subln, w_out_even, w_in_odd, gdn_conv_w, gdn_a_log, gdn_dt_bias, gdn_norm, mla_q_norm,
           mla_w_uq, mla_kv_norm, mla_w_uk, mla_w_uv, w_out_odd, router_w, router_b, moe_w_gu, moe_b_gu,
           moe_w_down, moe_b_down, ln_g, ln_b):
    bp, lp, _ = x_prompt.shape
    bs, ls, _ = x_sample.shape
    pos_p = jnp.arange(lp, dtype=F32)
    pos_s = PAST_LEN + jnp.arange(ls, dtype=F32)
    hp, hs = x_prompt, x_sample
    outs_p = {k: [] for k in ('a_re', 'a_im', 'b_k', 'b_v', 'c', 'c_conv', 'd_lat', 'd_rope')}
    outs_s = {k: [] for k in outs_p}
    for li in range(DEPTH):
        if li % 2 == 0:
            e = li // 2
            lam_init = 0.8 - 0.6 * math.exp(-0.3 * li)
            ew = (w_in_even[e], s5_lam_re[e], s5_lam_im[e], s5_log_dt[e], s5_b_re[e], s5_b_im[e],
                  s5_c_re[e], s5_c_im[e], s5_d[e], s5_w_glu[e], diff_lq1[e], diff_lk1[e], diff_lq2[e],
                  diff_lk2[e], diff_subln[e], w_out_even[e])
            z0 = jnp.zeros((bp, A_GROUPS, A_STATE), F32)
            mp, hr, hi, kk, vv = _even_mixer(hp, pos_p, z0, z0, None, lam_init, *ew)
            for key, val in zip(('a_re', 'a_im', 'b_k', 'b_v'), (hr, hi, kk, vv)):
                outs_p[key].append(val)
            ctx = (cache_b_k, cache_b_v, e, page_table)
            ms, hr, hi, kk, vv = _even_mixer(hs, pos_s, state_a_re[e], state_a_im[e], ctx, lam_init, *ew)
            for key, val in zip(('a_re', 'a_im', 'b_k', 'b_v'), (hr, hi, kk, vv)):
                outs_s[key].append(val)
        else:
            o = li // 2
            ow = (w_in_odd[o], gdn_conv_w[o], gdn_a_log[o], gdn_dt_bias[o], gdn_norm[o], mla_q_norm[o],
                  mla_w_uq[o], mla_kv_norm[o], mla_w_uk[o], mla_w_uv[o], w_out_odd[o])
            conv0 = jnp.zeros((bp, C_CONV - 1, C_CONV_CH), F32)
            s0 = jnp.zeros((bp, C_HEADS, C_DK, C_DV), F32)
            mp, S, cb, cl, kr = _odd_mixer(hp, pos_p, conv0, s0, None, *ow)
            for key, val in zip(('c', 'c_conv', 'd_lat', 'd_rope'), (S, cb, cl, kr)):
                outs_p[key].append(val)
            ctx = (cache_d_latent, cache_d_rope, o, page_table)
            ms, S, cb, cl, kr = _odd_mixer(hs, pos_s, state_c_conv[o], state_c[o], ctx, *ow)
            for key, val in zip(('c', 'c_conv', 'd_lat', 'd_rope'), (S, cb, cl, kr)):
                outs_s[key].append(val)
        post = (li, router_w, router_b, moe_w_gu, moe_b_gu, moe_w_down, moe_b_down, ln_g, ln_b)
        hp = _post_block(hp, mp, *post)
        hs = _post_block(hs, ms, *post)
    keys = ('a_re', 'a_im', 'b_k', 'b_v', 'c', 'c_conv', 'd_lat', 'd_rope')
    return ((hp, hs) + tuple(jnp.stack(outs_p[k]) for k in keys) + tuple(jnp.stack(outs_s[k]) for k in keys))
```

```python
import functools
import math

import jax
import jax.numpy as jnp
from jax import lax
from jax.experimental import pallas as pl
from jax.experimental.pallas import tpu as pltpu

D_MODEL = 1024
DEPTH = 2
PAST_LEN = 16384
PAGE_SIZE = 128
A_WIDTH = D_MODEL // 2
A_GROUP = 16
A_GROUPS = A_WIDTH // A_GROUP
A_STATE = 64
B_HEADS = 4
B_DK = 64
B_DV = 2 * B_DK
B_ROT = B_DK // 4
B_QK = B_HEADS * 2 * B_DK
C_HEADS = 4
C_DK = 128
C_DV = 128
C_CONV = 4
C_CHUNK = 64
C_QK = C_HEADS * C_DK
C_V = C_HEADS * C_DV
C_CONV_CH = 2 * C_QK + C_V
C_IN = C_CONV_CH + C_V + 2 * C_HEADS
D_HEADS = 4
D_NOPE = 128
D_ROPE = 32
D_DV = 128
D_Q_LORA = 384
D_KV_LORA = 256
ROPE_THETA = 500000.0
MLA_ROPE_THETA = 10000.0
N_EXPERTS = 32
TOP_K = 4
D_FF = 1024
SWIGLU_LIMIT = 7.0
SWIGLU_ALPHA = 1.702
DN_ALPHA = (2 * DEPTH) ** 0.25
EPS = 1e-6
NEG_BIG = -1e30

V7X_VMEM_BYTES = 64 * 1024 * 1024
VMEM_LIMIT = V7X_VMEM_BYTES * 7 // 8
BF16 = jnp.bfloat16
F32 = jnp.float32


def _cparams(sem):
    return pltpu.CompilerParams(dimension_semantics=sem, vmem_limit_bytes=VMEM_LIMIT)


def _mm_kernel(x_ref, w_ref, o_ref):
    o_ref[...] = jnp.dot(x_ref[...].astype(BF16), w_ref[...], preferred_element_type=F32)


def _mm(x, w_bf16, tm=256):
    m, k = x.shape
    n = w_bf16.shape[1]
    tm = min(tm, m)
    return pl.pallas_call(
        _mm_kernel,
        grid=(m // tm,),
        in_specs=[pl.BlockSpec((tm, k), lambda i: (i, 0)), pl.BlockSpec((k, n), lambda i: (0, 0))],
        out_specs=pl.BlockSpec((tm, n), lambda i: (i, 0)),
        out_shape=jax.ShapeDtypeStruct((m, n), F32),
        compiler_params=_cparams(("parallel",)),
        name="mm",
    )(x, w_bf16)


S5_HALF_IN = A_WIDTH // 2
S5_HALF_ST = A_GROUPS * A_STATE // 2
S5_STATES = A_GROUPS * A_STATE


def _s5_kernel(u_ref, h0r_ref, h0i_ref, a_ref, win_ref, cre_ref, cim_ref, d_ref, wglu_ref, *rest, nb, steps,
               batch_major):
    if batch_major:
        perm_ref, perm_t_ref = rest[:2]
        rest = rest[2:]
    y_ref, hr_out, hi_out, x_sc, hr_sc, hi_sc = rest
    j = pl.program_id(0)

    @pl.when(j == 0)
    def _():
        hr_sc[...] = h0r_ref[...]
        hi_sc[...] = h0i_ref[...]

    rows = steps * nb
    u = u_ref[...].reshape(rows, A_WIDTH)
    ub = u.astype(BF16)
    if batch_major:
        ub = jnp.dot(perm_ref[...], ub, preferred_element_type=F32).astype(BF16)
    for k in range(2):
        xk = jnp.dot(ub[:, k * S5_HALF_IN:(k + 1) * S5_HALF_IN], win_ref[k], preferred_element_type=F32)
        x_sc[:, k * S5_HALF_ST:(k + 1) * S5_HALF_ST] = xk[:, :S5_HALF_ST]
        x_sc[:, S5_STATES + k * S5_HALF_ST:S5_STATES + (k + 1) * S5_HALF_ST] = xk[:, S5_HALF_ST:]
    ar = a_ref[0:1, :]
    ai = a_ref[1:2, :]

    def body(t, carry):
        r = pl.ds(pl.multiple_of(t * nb, nb), nb)
        hr = hr_sc[...]
        hi = hi_sc[...]
        nr = ar * hr - ai * hi + x_sc[r, 0:S5_STATES]
        ni = ar * hi + ai * hr + x_sc[r, S5_STATES:2 * S5_STATES]
        x_sc[r, 0:S5_STATES] = nr
        x_sc[r, S5_STATES:2 * S5_STATES] = ni
        hr_sc[...] = nr
        hi_sc[...] = ni
        return carry

    lax.fori_loop(0, steps, body, 0)
    ys = []
    for k in range(2):
        hk_r = x_sc[:, k * S5_HALF_ST:(k + 1) * S5_HALF_ST].astype(BF16)
        hk_i = x_sc[:, S5_STATES + k * S5_HALF_ST:S5_STATES + (k + 1) * S5_HALF_ST].astype(BF16)
        yk = (jnp.dot(hk_r, cre_ref[k], preferred_element_type=F32)
              - jnp.dot(hk_i, cim_ref[k], preferred_element_type=F32))
        ys.append(yk)
    y = jnp.concatenate(ys, axis=-1)
    if batch_major:
        y = _dot_sel(perm_t_ref[...], y)
    y = y + d_ref[...] * u
    y = jax.nn.gelu(y)
    y = y * jax.nn.sigmoid(jnp.dot(y.astype(BF16), wglu_ref[...], preferred_element_type=F32))
    y_ref[...] = y.reshape(y_ref.shape)

    @pl.when(j == pl.num_programs(0) - 1)
    def _():
        hr_out[...] = hr_sc[...]
        hi_out[...] = hi_sc[...]


def _s5_weights(lam_re, lam_im, log_dt, b_re, b_im, c_re, c_im):
    dt = jnp.exp(log_dt)[:, None]
    mag = jnp.exp(lam_re * dt)
    ar = mag * jnp.cos(lam_im * dt)
    ai = mag * jnp.sin(lam_im * dt)
    den = lam_re * lam_re + lam_im * lam_im
    fr = ((ar - 1.0) * lam_re + ai * lam_im) / den
    fi = (ai * lam_re - (ar - 1.0) * lam_im) / den
    bbr = fr[..., None] * b_re - fi[..., None] * b_im
    bbi = fr[..., None] * b_im + fi[..., None] * b_re
    eye = jnp.eye(A_GROUPS, dtype=F32)

    def in_map(bb):
        return jnp.einsum('gh,gnp->gphn', eye, bb).reshape(A_WIDTH, S5_STATES)

    def out_map(c):
        return jnp.einsum('gh,gpn->gnhp', eye, c).reshape(S5_STATES, A_WIDTH)

    wr, wi = in_map(bbr), in_map(bbi)
    win = jnp.stack([jnp.concatenate([w[k * S5_HALF_IN:(k + 1) * S5_HALF_IN, k * S5_HALF_ST:(k + 1) * S5_HALF_ST]
                                      for w in (wr, wi)], axis=1) for k in range(2)]).astype(BF16)
    cr, ci = out_map(c_re), out_map(c_im)
    cre = jnp.stack([cr[k * S5_HALF_ST:(k + 1) * S5_HALF_ST, k * S5_HALF_IN:(k + 1) * S5_HALF_IN]
                     for k in range(2)]).astype(BF16)
    cim = jnp.stack([ci[k * S5_HALF_ST:(k + 1) * S5_HALF_ST, k * S5_HALF_IN:(k + 1) * S5_HALF_IN]
                     for k in range(2)]).astype(BF16)
    a = jnp.stack([ar.reshape(-1), ai.reshape(-1)])
    return a, win, cre, cim


def _s5_glu(u, h0_re, h0_im, weights, d_skip, w_glu_bf16, steps, batch_major):
    if batch_major:
        nb, seq, _ = u.shape
        blk, idx, out_full = (nb, steps, A_WIDTH), (lambda j: (0, j, 0)), (nb, seq, A_WIDTH)
    else:
        seq, nb, _ = u.shape
        blk, idx, out_full = (steps, nb, A_WIDTH), (lambda j: (j, 0, 0)), (seq, nb, A_WIDTH)
    a, win, cre, cim = weights
    rows = steps * nb
    const = lambda *shape: pl.BlockSpec(shape, lambda j: (0,) * len(shape))
    extra_specs, extra_args = [], []
    if batch_major:
        r = jnp.arange(rows, dtype=jnp.int32)
        src = (r % nb) * steps + r // nb
        perm = (src[:, None] == r[None, :]).astype(BF16)
        extra_specs, extra_args = [const(rows, rows), const(rows, rows)], [perm, perm.T]
    return pl.pallas_call(
        functools.partial(_s5_kernel, nb=nb, steps=steps, batch_major=batch_major),
        grid=(seq // steps,),
        in_specs=[pl.BlockSpec(blk, idx),
                  const(nb, S5_STATES), const(nb, S5_STATES), const(2, S5_STATES),
                  const(2, S5_HALF_IN, 2 * S5_HALF_ST), const(2, S5_HALF_ST, S5_HALF_IN),
                  const(2, S5_HALF_ST, S5_HALF_IN), const(1, A_WIDTH), const(A_WIDTH, A_WIDTH)] + extra_specs,
        out_specs=[pl.BlockSpec(blk, idx), const(nb, S5_STATES), const(nb, S5_STATES)],
        out_shape=[jax.ShapeDtypeStruct(out_full, F32), jax.ShapeDtypeStruct((nb, S5_STATES), F32),
                   jax.ShapeDtypeStruct((nb, S5_STATES), F32)],
        scratch_shapes=[pltpu.VMEM((rows, 2 * S5_STATES), F32), pltpu.VMEM((nb, S5_STATES), F32),
                        pltpu.VMEM((nb, S5_STATES), F32)],
        compiler_params=_cparams(("arbitrary",)), name="s5_glu",
    )(u, h0_re, h0_im, a, win, cre, cim, d_skip.reshape(1, A_WIDTH), w_glu_bf16, *extra_args)


GDN_BA_COLS = 128
HI = lax.Precision.HIGHEST


def _gdn_pre_kernel(x_ref, conv0_ref, ba_ref, w_ref, alog_ref, dtb_ref, q_ref, k_ref, v_ref, bg_ref, xp_sc,
                    *, tm, tmo):
    j = pl.program_id(1)

    @pl.when(j == 0)
    def _():
        xp_sc[0:8, :] = conv0_ref[0]

    @pl.when(j > 0)
    def _():
        xp_sc[0:8, :] = xp_sc[tm:tm + 8, :]

    xp_sc[8:8 + tm, :] = x_ref[0]
    conv = w_ref[0:1, :] * xp_sc[pl.ds(5, tm), :]
    for jj in range(1, C_CONV):
        conv = conv + w_ref[jj:jj + 1, :] * xp_sc[pl.ds(5 + jj, tm), :]
    act = conv * jax.nn.sigmoid(conv)
    pad = tmo - tm

    def put(ref, val):
        if pad:
            val = jnp.concatenate([val, jnp.zeros((pad, val.shape[1]), F32)], axis=0)
        ref[0] = val

    qs, ks = [], []
    for h in range(C_HEADS):
        qh = act[:, h * C_DK:(h + 1) * C_DK]
        kh = act[:, C_QK + h * C_DK:C_QK + (h + 1) * C_DK]
        qs.append(qh * lax.rsqrt(jnp.sum(qh * qh, axis=-1, keepdims=True) + EPS) * (C_DK ** -0.5))
        ks.append(kh * lax.rsqrt(jnp.sum(kh * kh, axis=-1, keepdims=True) + EPS))
    put(q_ref, jnp.concatenate(qs, axis=-1))
    put(k_ref, jnp.concatenate(ks, axis=-1))
    put(v_ref, act[:, 2 * C_QK:])
    ba = ba_ref[0]
    lane = lax.broadcasted_iota(jnp.int32, ba.shape, 1)
    beta = jax.nn.sigmoid(ba)
    g = -jnp.exp(alog_ref[...]) * jax.nn.softplus(ba + dtb_ref[...])
    bg = jnp.where(lane < C_HEADS, beta, jnp.where(lane < 2 * C_HEADS, g, 0.0))
    put(bg_ref, bg)


def _gdn_pre(proj3, conv0, conv_w, a_log, dt_bias, tm, tmo):
    bsz, seq, _ = proj3.shape
    nblk = seq // tm
    conv0_p = jnp.pad(conv0, ((0, 0), (8 - (C_CONV - 1), 0), (0, 0)))
    vec = lambda x: jnp.zeros((1, GDN_BA_COLS), F32).at[0, C_HEADS:2 * C_HEADS].set(x)
    ba_blk = (C_CONV_CH + C_V) // GDN_BA_COLS
    out_len = nblk * tmo
    return pl.pallas_call(
        functools.partial(_gdn_pre_kernel, tm=tm, tmo=tmo),
        grid=(bsz, nblk),
        in_specs=[pl.BlockSpec((1, tm, C_CONV_CH), lambda b, j: (b, j, 0)),
                  pl.BlockSpec((1, 8, C_CONV_CH), lambda b, j: (b, 0, 0)),
                  pl.BlockSpec((1, tm, GDN_BA_COLS), lambda b, j: (b, j, ba_blk)),
                  pl.BlockSpec((C_CONV, C_CONV_CH), lambda b, j: (0, 0)),
                  pl.BlockSpec((1, GDN_BA_COLS), lambda b, j: (0, 0)),
                  pl.BlockSpec((1, GDN_BA_COLS), lambda b, j: (0, 0))],
        out_specs=[pl.BlockSpec((1, tmo, C_QK), lambda b, j: (b, j, 0)),
                   pl.BlockSpec((1, tmo, C_QK), lambda b, j: (b, j, 0)),
                   pl.BlockSpec((1, tmo, C_V), lambda b, j: (b, j, 0)),
                   pl.BlockSpec((1, tmo, GDN_BA_COLS), lambda b, j: (b, j, 0))],
        out_shape=[jax.ShapeDtypeStruct((bsz, out_len, C_QK), F32), jax.ShapeDtypeStruct((bsz, out_len, C_QK), F32),
                   jax.ShapeDtypeStruct((bsz, out_len, C_V), F32),
                   jax.ShapeDtypeStruct((bsz, out_len, GDN_BA_COLS), F32)],
        scratch_shapes=[pltpu.VMEM((tm + 16, C_CONV_CH), F32)],
        compiler_params=_cparams(("parallel", "arbitrary")), name="gdn_pre",
    )(proj3, conv0_p, proj3, conv_w, vec(a_log), vec(dt_bias))


def _dot_hi(a, b):
    return jnp.dot(a, b, preferred_element_type=F32, precision=HI)


def _dot_bf(a, b):
    return jnp.dot(a.astype(BF16), b.astype(BF16), preferred_element_type=F32)


_NN = (((1,), (0,)), ((), ()))
_NT = (((1,), (1,)), ((), ()))


def _split2(x):
    hi = x.astype(BF16)
    return hi, (x - hi.astype(F32)).astype(BF16)


def _dot3(a2, b2, dims=_NN):
    d = lambda x, y: lax.dot_general(x, y, dims, preferred_element_type=F32)
    return d(a2[0], b2[0]) + (d(a2[0], b2[1]) + d(a2[1], b2[0]))


def _dot_sel(sel_bf16, b, dims=_NN):
    b1 = b.astype(BF16)
    r1 = b - b1.astype(F32)
    b2 = r1.astype(BF16)
    b3 = (r1 - b2.astype(F32)).astype(BF16)
    d = lambda y: lax.dot_general(sel_bf16, y, dims, preferred_element_type=F32)
    return d(b1) + (d(b2) + d(b3))


def _gdn_chunk_kernel(q_ref, k_ref, v_ref, bg_ref, z_ref, s0_ref, ng_ref, y_ref, s_out, s_sc, *, chunk, nchunk, zlen):
    j = pl.program_id(1)

    @pl.when(j == 0)
    def _():
        s_sc[...] = s0_ref[0]

    ri = lax.broadcasted_iota(jnp.int32, (chunk, chunk), 0)
    ci = lax.broadcasted_iota(jnp.int32, (chunk, chunk), 1)
    causal = ci <= ri
    strict = ci < ri
    ltri = causal.astype(BF16)
    eye = (ci == ri).astype(F32)
    e0 = (lax.broadcasted_iota(jnp.int32, (chunk, C_DK), 1) == 0).astype(BF16)
    n_dbl = chunk.bit_length() - 2

    def do_chunk(c, carry):
        r = pl.ds(pl.multiple_of(c * chunk, chunk), chunk)
        bg = bg_ref[0, r, :]
        for h in range(C_HEADS):
            q = q_ref[0, r, h * C_DK:(h + 1) * C_DK]
            k = k_ref[0, r, h * C_DK:(h + 1) * C_DK]
            v = v_ref[0, r, h * C_DV:(h + 1) * C_DV]
            beta = bg[:, h:h + 1]
            g = bg[:, C_HEADS + h:C_HEADS + h + 1]
            gcol = _dot_sel(ltri, jnp.broadcast_to(g, (chunk, C_DK)))
            grow = _dot_sel(e0, gcol, _NT)
            decay = jnp.where(causal, jnp.exp(jnp.where(causal, gcol[:, :chunk] - grow, 0.0)), 0.0)
            kb = k * beta
            vb = v * beta
            k_2 = _split2(k)
            m = jnp.where(strict, _dot3(_split2(kb), k_2, _NT) * decay, 0.0)
            tinv = eye - m
            mp = m.astype(BF16)
            for _ in range(n_dbl):
                mp = jnp.dot(mp, mp, preferred_element_type=F32).astype(BF16)
                tinv = tinv + jnp.dot(tinv.astype(BF16), mp, preferred_element_type=F32)
            tinv_b = tinv.astype(BF16)
            u = _dot_bf(tinv_b, vb)
            w = _dot_bf(tinv_b, kb * jnp.exp(gcol))
            s = s_sc[h]
            s_b = s.astype(BF16)
            v_new = u - _dot_bf(w, s_b)
            a_intra = jnp.where(causal, _dot3(_split2(q), k_2, _NT), 0.0) * decay
            o = _dot_bf(q * jnp.exp(gcol), s_b) + _dot_bf(a_intra, v_new)
            g_last = gcol[chunk - 1:chunk, :]
            k2 = k * jnp.exp(g_last - gcol)
            s_sc[h] = s * jnp.exp(g_last) + lax.dot_general(k2.astype(BF16), v_new.astype(BF16),
                                                            (((0,), (0,)), ((), ())), preferred_element_type=F32)
            o = o * lax.rsqrt(jnp.mean(o * o, axis=-1, keepdims=True) + EPS) * ng_ref[...]
            z = z_ref[0, pl.ds(pl.multiple_of(c * chunk, chunk), zlen), h * C_DV:(h + 1) * C_DV]
            y_ref[0, pl.ds(pl.multiple_of(c * chunk, chunk), zlen), h * C_DV:(h + 1) * C_DV] = (
                o[:zlen] * (z * jax.nn.sigmoid(z)))
        return carry

    lax.fori_loop(0, nchunk, do_chunk, 0)

    @pl.when(j == pl.num_programs(1) - 1)
    def _():
        s_out[0] = s_sc[...]


def _gdn_chunks(q, k, v, bg, proj3, s0, norm_g, chunk, nchunk, zlen):
    bsz, out_len, _ = q.shape
    rb = chunk * nchunk
    nblk = out_len // rb
    zrows = zlen if zlen < chunk else rb
    z_blk = C_CONV_CH // C_V
    row = lambda w: pl.BlockSpec((1, rb, w), lambda b, j: (b, j, 0))
    return pl.pallas_call(
        functools.partial(_gdn_chunk_kernel, chunk=chunk, nchunk=nchunk, zlen=min(zlen, chunk)),
        grid=(bsz, nblk),
        in_specs=[row(C_QK), row(C_QK), row(C_V), row(GDN_BA_COLS),
                  pl.BlockSpec((1, zrows, C_V), lambda b, j: (b, j, z_blk)),
                  pl.BlockSpec((1, C_HEADS, C_DK, C_DV), lambda b, j: (b, 0, 0, 0)),
                  pl.BlockSpec((1, C_DV), lambda b, j: (0, 0))],
        out_specs=[pl.BlockSpec((1, zrows, C_V), lambda b, j: (b, j, 0)),
                   pl.BlockSpec((1, C_HEADS, C_DK, C_DV), lambda b, j: (b, 0, 0, 0))],
        out_shape=[jax.ShapeDtypeStruct((bsz, nblk * zrows, C_V), F32),
                   jax.ShapeDtypeStruct((bsz, C_HEADS, C_DK, C_DV), F32)],
        scratch_shapes=[pltpu.VMEM((C_HEADS, C_DK, C_DV), F32)],
        compiler_params=_cparams(("parallel", "arbitrary")), name="gdn_chunks",
    )(q, k, v, bg, proj3, s0, norm_g.reshape(1, C_DV))


def _tri_pairs(n):
    qi = [i for i in range(n) for _ in range(i + 1)]
    kj = [j for i in range(n) for j in range(i + 1)]
    return jnp.asarray(qi, jnp.int32), jnp.asarray(kj, jnp.int32)


def _softmax_update(s, v_bf16, m_ref, l_ref, acc_ref, idx):
    m_prev = m_ref[idx]
    m_new = jnp.maximum(m_prev, jnp.max(s, axis=-1, keepdims=True))
    corr = jnp.exp(m_prev - m_new)
    p = jnp.exp(s - m_new)
    l_ref[idx] = corr * l_ref[idx] + jnp.sum(p, axis=-1, keepdims=True)
    acc_ref[idx] = corr * acc_ref[idx] + jnp.dot(p.astype(BF16), v_bf16, preferred_element_type=F32)
    m_ref[idx] = m_new


def _diff_attn_kernel(qi_ref, kj_ref, lam_ref, q_ref, k_ref, v_ref, g_ref, o_ref, m_ref, l_ref, acc_ref,
                      *, blk, out_scale):
    p = pl.program_id(2)
    qi = qi_ref[p]
    kj = kj_ref[p]

    @pl.when(kj == 0)
    def _():
        m_ref[...] = jnp.full(m_ref.shape, NEG_BIG, F32)
        l_ref[...] = jnp.zeros(l_ref.shape, F32)
        acc_ref[...] = jnp.zeros(acc_ref.shape, F32)

    def step(masked):
        q = (q_ref[...] * (B_DK ** -0.5)).astype(BF16)
        k = k_ref[...].astype(BF16)
        v = v_ref[...].astype(BF16)
        for c in range(2):
            s = lax.dot_general(q[:, c * B_DK:(c + 1) * B_DK], k[:, c * B_DK:(c + 1) * B_DK],
                                (((1,), (1,)), ((), ())), preferred_element_type=F32)
            if masked:
                row = lax.broadcasted_iota(jnp.int32, s.shape, 0)
                col = lax.broadcasted_iota(jnp.int32, s.shape, 1)
                s = jnp.where(col <= row, s, NEG_BIG)
            _softmax_update(s, v, m_ref, l_ref, acc_ref, c)

    @pl.when(kj < qi)
    def _():
        step(False)

    @pl.when(kj == qi)
    def _():
        step(True)
        lam = lam_ref[0, 0]
        o0 = acc_ref[0] / l_ref[0]
        o1 = acc_ref[1] / l_ref[1]
        y = o0 - lam * o1
        y = y * lax.rsqrt(jnp.mean(y * y, axis=-1, keepdims=True) + EPS)
        o_ref[...] = y * g_ref[...] * out_scale


def _diff_attn_prompt(q, k, v, lam, subln, bsz, seq, lam_init, blk=512):
    nq = seq // blk
    qi, kj = _tri_pairs(nq)
    kern = functools.partial(_diff_attn_kernel, blk=blk, out_scale=1.0 - lam_init)
    grid_spec = pltpu.PrefetchScalarGridSpec(
        num_scalar_prefetch=2,
        grid=(bsz, B_HEADS, qi.shape[0]),
        in_specs=[
            pl.BlockSpec(memory_space=pltpu.SMEM),
            pl.BlockSpec((blk, B_DV), lambda b, h, p, qi, kj: (b * nq + qi[p], h)),
            pl.BlockSpec((blk, B_DV), lambda b, h, p, qi, kj: (b * nq + kj[p], h)),
            pl.BlockSpec((blk, B_DV), lambda b, h, p, qi, kj: (b * nq + kj[p], h)),
            pl.BlockSpec((1, B_DV), lambda b, h, p, qi, kj: (0, 0)),
        ],
        out_specs=pl.BlockSpec((blk, B_DV), lambda b, h, p, qi, kj: (b * nq + qi[p], h)),
        scratch_shapes=[pltpu.VMEM((2, blk, 1), F32), pltpu.VMEM((2, blk, 1), F32),
                        pltpu.VMEM((2, blk, B_DV), F32)],
    )
    return pl.pallas_call(
        kern, grid_spec=grid_spec, out_shape=jax.ShapeDtypeStruct(q.shape, F32),
        compiler_params=_cparams(("parallel", "parallel", "arbitrary")), name="diff_attn_prompt",
    )(qi, kj, lam.reshape(1, 1), q, k, v, subln.reshape(1, B_DV))


def _mla_attn_kernel(qi_ref, kj_ref, ql_ref, qr_ref, c_ref, kr_ref, wuv_ref, o_ref, m_ref, l_ref, acc_ref, *, blk):
    p = pl.program_id(1)
    qi = qi_ref[p]
    kj = kj_ref[p]
    scale = (D_NOPE + D_ROPE) ** -0.5

    @pl.when(kj == 0)
    def _():
        m_ref[...] = jnp.full(m_ref.shape, NEG_BIG, F32)
        l_ref[...] = jnp.zeros(l_ref.shape, F32)
        acc_ref[...] = jnp.zeros(acc_ref.shape, F32)

    def step(masked):
        c = c_ref[...].astype(BF16)
        kr = kr_ref[...].astype(BF16)
        for h in range(D_HEADS):
            ql = (ql_ref[:, h * D_KV_LORA:(h + 1) * D_KV_LORA] * scale).astype(BF16)
            qr = (qr_ref[:, h * D_ROPE:(h + 1) * D_ROPE] * scale).astype(BF16)
            s = (lax.dot_general(ql, c, (((1,), (1,)), ((), ())), preferred_element_type=F32)
                 + lax.dot_general(qr, kr, (((1,), (1,)), ((), ())), preferred_element_type=F32))
            if masked:
                row = lax.broadcasted_iota(jnp.int32, s.shape, 0)
                col = lax.broadcasted_iota(jnp.int32, s.shape, 1)
                s = jnp.where(col <= row, s, NEG_BIG)
            _softmax_update(s, c, m_ref, l_ref, acc_ref, h)

    @pl.when(kj < qi)
    def _():
        step(False)

    @pl.when(kj == qi)
    def _():
        step(True)
        for h in range(D_HEADS):
            o = (acc_ref[h] / l_ref[h]).astype(BF16)
            o_ref[:, h * D_DV:(h + 1) * D_DV] = jnp.dot(o, wuv_ref[h], preferred_element_type=F32)


def _mla_attn_prompt(q_lat, q_rope, c, kr, w_uv_bf16, bsz, seq, blk=512):
    nq = seq // blk
    qi, kj = _tri_pairs(nq)
    grid_spec = pltpu.PrefetchScalarGridSpec(
        num_scalar_prefetch=2,
        grid=(bsz, qi.shape[0]),
        in_specs=[
            pl.BlockSpec((blk, D_HEADS * D_KV_LORA), lambda b, p, qi, kj: (b * nq + qi[p], 0)),
            pl.BlockSpec((blk, D_HEADS * D_ROPE), lambda b, p, qi, kj: (b * nq + qi[p], 0)),
            pl.BlockSpec((blk, D_KV_LORA), lambda b, p, qi, kj: (b * nq + kj[p], 0)),
            pl.BlockSpec((blk, D_ROPE), lambda b, p, qi, kj: (b * nq + kj[p], 0)),
            pl.BlockSpec((D_HEADS, D_KV_LORA, D_DV), lambda b, p, qi, kj: (0, 0, 0)),
        ],
        out_specs=pl.BlockSpec((blk, D_HEADS * D_DV), lambda b, p, qi, kj: (b * nq + qi[p], 0)),
        scratch_shapes=[pltpu.VMEM((D_HEADS, blk, 1), F32), pltpu.VMEM((D_HEADS, blk, 1), F32),
                        pltpu.VMEM((D_HEADS, blk, D_KV_LORA), F32)],
    )
    return pl.pallas_call(
        functools.partial(_mla_attn_kernel, blk=blk), grid_spec=grid_spec,
        out_shape=jax.ShapeDtypeStruct((q_lat.shape[0], D_HEADS * D_DV), F32),
        compiler_params=_cparams(("parallel", "arbitrary")), name="mla_attn_prompt",
    )(qi, kj, q_lat, q_rope, c, kr, w_uv_bf16)


PAGES_PER_STEP = 16
NEW_PAD = 8


def _page_spec(width, i, n_pages, base):
    return pl.BlockSpec((1, PAGE_SIZE, width),
                        lambda b, p, pt: (pt[b * n_pages + p * PAGES_PER_STEP + i] + base, 0, 0))


def _flash_update(s, v_bf16, m_ref, l_ref, acc_ref):
    m_prev = m_ref[...]
    m_new = jnp.maximum(m_prev, jnp.max(s, axis=-1, keepdims=True))
    corr = jnp.exp(m_prev - m_new)
    p = jnp.exp(s - m_new)
    l_ref[...] = corr * l_ref[...] + jnp.sum(p, axis=-1, keepdims=True)
    acc_ref[...] = corr * acc_ref[...] + jnp.dot(p.astype(BF16), v_bf16, preferred_element_type=F32)
    m_ref[...] = m_new


def _flash_update_pages(s, values, m_ref, l_ref, acc_ref, width=PAGE_SIZE):
    m_prev = m_ref[...]
    m_new = jnp.maximum(m_prev, jnp.max(s, axis=-1, keepdims=True))
    corr = jnp.exp(m_prev - m_new)
    p = jnp.exp(s - m_new)
    l_ref[...] = corr * l_ref[...] + jnp.sum(p, axis=-1, keepdims=True)
    pb = p.astype(BF16)
    pv = jnp.dot(pb[:, :width], values[0].astype(BF16), preferred_element_type=F32)
    for i in range(1, len(values)):
        pv = pv + jnp.dot(pb[:, i * width:(i + 1) * width], values[i].astype(BF16), preferred_element_type=F32)
    acc_ref[...] = corr * acc_ref[...] + pv
    m_ref[...] = m_new


def _self_mask(s, t_new):
    row_t = lax.broadcasted_iota(jnp.int32, s.shape, 0) % t_new
    col = lax.broadcasted_iota(jnp.int32, s.shape, 1)
    return jnp.where(col <= row_t, s, NEG_BIG)


def _diff_paged_kernel(pt_ref, lam_ref, q_ref, kn_ref, vn_ref, g_ref, *refs, out_scale):
    k_refs = refs[:PAGES_PER_STEP]
    v_refs = refs[PAGES_PER_STEP:2 * PAGES_PER_STEP]
    o_ref, m_ref, l_ref, acc_ref = refs[2 * PAGES_PER_STEP:]
    p = pl.program_id(1)

    @pl.when(p == 0)
    def _():
        m_ref[...] = jnp.full(m_ref.shape, NEG_BIG, F32)
        l_ref[...] = jnp.zeros(l_ref.shape, F32)
        acc_ref[...] = jnp.zeros(acc_ref.shape, F32)

    q = (q_ref[0] * (B_DK ** -0.5)).astype(BF16)
    hrows = 2 * NEW_PAD
    s = jnp.concatenate([lax.dot_general(q, k_refs[i][0].astype(BF16), _NT, preferred_element_type=F32)
                         for i in range(PAGES_PER_STEP)], axis=-1)
    row_h = lax.broadcasted_iota(jnp.int32, s.shape, 0) // hrows
    col_h = lax.broadcasted_iota(jnp.int32, s.shape, 1) % B_HEADS
    s = jnp.where(row_h == col_h, s, NEG_BIG)
    _flash_update_pages(s, [r[0] for r in v_refs], m_ref, l_ref, acc_ref, PAGE_SIZE * B_HEADS)

    @pl.when(p == pl.num_programs(1) - 1)
    def _():
        for h in range(B_HEADS):
            rs = pl.ds(h * hrows, hrows)
            qh = q[h * hrows:(h + 1) * hrows]
            kn = kn_ref[0, :, h * B_DV:(h + 1) * B_DV].astype(BF16)
            s = lax.dot_general(qh, kn, _NT, preferred_element_type=F32)
            _flash_update(_self_mask(s, NEW_PAD), vn_ref[0, :, h * B_DV:(h + 1) * B_DV].astype(BF16),
                          m_ref.at[rs], l_ref.at[rs], acc_ref.at[rs])
            o = acc_ref[rs, :] / l_ref[rs, :]
            y = o[:NEW_PAD] - lam_ref[0, 0] * o[NEW_PAD:]
            o_ref[0, :, h * B_DV:(h + 1) * B_DV] = (
                y * lax.rsqrt(jnp.mean(y * y, axis=-1, keepdims=True) + EPS) * g_ref[...] * out_scale)


def _diff_attn_paged(q, k, v, cache_k, cache_v, layer, page_table, lam, subln, lam_init):
    bd, t_new = q.shape[:2]
    n_pages = page_table.shape[1]
    qe = jnp.einsum('bthcd,ce->bhcted', q, jnp.eye(2, dtype=F32))
    qe = jnp.pad(qe, ((0, 0), (0, 0), (0, 0), (0, NEW_PAD - t_new), (0, 0), (0, 0)))
    rows = B_HEADS * 2 * NEW_PAD
    qe = qe.reshape(bd, rows, 2 * B_DK)
    padn = lambda x: jnp.pad(x.reshape(bd, t_new, -1), ((0, 0), (0, NEW_PAD - t_new), (0, 0)))
    width = B_HEADS * B_DV
    per_b = lambda r, w: pl.BlockSpec((1, r, w), lambda b, p, pt: (b, 0, 0))
    n_pool = cache_k.shape[1]
    ck = cache_k.reshape(-1, PAGE_SIZE * B_HEADS, 2 * B_DK)
    cv = cache_v.reshape(-1, PAGE_SIZE * B_HEADS, B_DV)
    page = lambda i, w: pl.BlockSpec(
        (1, PAGE_SIZE * B_HEADS, w),
        lambda b, p, pt: (layer * n_pool + pt[b * n_pages + p * PAGES_PER_STEP + i], 0, 0))
    grid_spec = pltpu.PrefetchScalarGridSpec(
        num_scalar_prefetch=1,
        grid=(bd, n_pages // PAGES_PER_STEP),
        in_specs=[pl.BlockSpec(memory_space=pltpu.SMEM), per_b(rows, 2 * B_DK), per_b(NEW_PAD, width),
                  per_b(NEW_PAD, width), pl.BlockSpec((1, B_DV), lambda b, p, pt: (0, 0))]
        + [page(i, 2 * B_DK) for i in range(PAGES_PER_STEP)] + [page(i, B_DV) for i in range(PAGES_PER_STEP)],
        out_specs=per_b(NEW_PAD, width),
        scratch_shapes=[pltpu.VMEM((rows, 1), F32), pltpu.VMEM((rows, 1), F32), pltpu.VMEM((rows, B_DV), F32)],
    )
    y = pl.pallas_call(
        functools.partial(_diff_paged_kernel, out_scale=1.0 - lam_init), grid_spec=grid_spec,
        out_shape=jax.ShapeDtypeStruct((bd, NEW_PAD, width), F32),
        compiler_params=_cparams(("parallel", "arbitrary")), name="diff_attn_paged",
    )(page_table.reshape(-1), lam.reshape(1, 1), qe, padn(k), padn(v), subln.reshape(1, B_DV),
      *([ck] * PAGES_PER_STEP), *([cv] * PAGES_PER_STEP))
    return y[:, :t_new]


def _mla_paged_kernel(pt_ref, ql_ref, qr_ref, cn_ref, rn_ref, *refs, t_new):
    c_refs = refs[:PAGES_PER_STEP]
    r_refs = refs[PAGES_PER_STEP:2 * PAGES_PER_STEP]
    o_ref, m_ref, l_ref, acc_ref = refs[2 * PAGES_PER_STEP:]
    p = pl.program_id(1)
    scale = (D_NOPE + D_ROPE) ** -0.5

    @pl.when(p == 0)
    def _():
        m_ref[...] = jnp.full(m_ref.shape, NEG_BIG, F32)
        l_ref[...] = jnp.zeros(l_ref.shape, F32)
        acc_ref[...] = jnp.zeros(acc_ref.shape, F32)

    ql = (ql_ref[0] * scale).astype(BF16)
    qr = (qr_ref[0] * scale).astype(BF16)
    nt = (((1,), (1,)), ((), ()))

    def scores(c_bf16, r_bf16):
        return (lax.dot_general(ql, c_bf16, nt, preferred_element_type=F32)
                + lax.dot_general(qr, r_bf16, nt, preferred_element_type=F32))

    cs = [c_refs[i][0].astype(BF16) for i in range(PAGES_PER_STEP)]
    s = jnp.concatenate([scores(cs[i], r_refs[i][0].astype(BF16)) for i in range(PAGES_PER_STEP)], axis=-1)
    _flash_update_pages(s, cs, m_ref, l_ref, acc_ref)

    @pl.when(p == pl.num_programs(1) - 1)
    def _():
        c = cn_ref[0].astype(BF16)
        _flash_update(_self_mask(scores(c, rn_ref[0].astype(BF16)), t_new), c, m_ref, l_ref, acc_ref)
        o_ref[0] = acc_ref[...] / l_ref[...]


def _mla_attn_paged(q_lat, q_rope, c, kr, cache_lat, cache_rope, layer, page_table):
    bd, t_new = q_lat.shape[:2]
    n_pool = cache_lat.shape[1]
    n_pages = page_table.shape[1]
    cl = cache_lat.reshape(-1, PAGE_SIZE, D_KV_LORA)
    cr = cache_rope.reshape(-1, PAGE_SIZE, D_ROPE)
    rows = D_HEADS * t_new
    ql = q_lat.transpose(0, 2, 1, 3).reshape(bd, rows, D_KV_LORA)
    qr = q_rope.transpose(0, 2, 1, 3).reshape(bd, rows, D_ROPE)
    padn = lambda x: jnp.pad(x, ((0, 0), (0, NEW_PAD - t_new), (0, 0)))
    per_b = lambda r, w: pl.BlockSpec((1, r, w), lambda b, p, pt: (b, 0, 0))
    grid_spec = pltpu.PrefetchScalarGridSpec(
        num_scalar_prefetch=1,
        grid=(bd, n_pages // PAGES_PER_STEP),
        in_specs=[per_b(rows, D_KV_LORA), per_b(rows, D_ROPE), per_b(NEW_PAD, D_KV_LORA), per_b(NEW_PAD, D_ROPE)]
        + [_page_spec(D_KV_LORA, i, n_pages, layer * n_pool) for i in range(PAGES_PER_STEP)]
        + [_page_spec(D_ROPE, i, n_pages, layer * n_pool) for i in range(PAGES_PER_STEP)],
        out_specs=per_b(rows, D_KV_LORA),
        scratch_shapes=[pltpu.VMEM((rows, 1), F32), pltpu.VMEM((rows, 1), F32), pltpu.VMEM((rows, D_KV_LORA), F32)],
    )
    o = pl.pallas_call(
        functools.partial(_mla_paged_kernel, t_new=t_new), grid_spec=grid_spec,
        out_shape=jax.ShapeDtypeStruct((bd, rows, D_KV_LORA), F32),
        compiler_params=_cparams(("parallel", "arbitrary")), name="mla_attn_paged",
    )(page_table.reshape(-1), ql, qr, padn(c), padn(kr), *([cl] * PAGES_PER_STEP), *([cr] * PAGES_PER_STEP))
    return o.reshape(bd, D_HEADS, t_new, D_KV_LORA).transpose(0, 2, 1, 3)


ROUTER_PAD = 128


def _out_ln_router_kernel(xa_ref, xb_ref, wa_ref, wb_ref, h_ref, g_ref, b_ref, rw_ref, rb_ref, o_ref, lg_ref):
    mix = (jnp.dot(xa_ref[...].astype(BF16), wa_ref[...], preferred_element_type=F32)
           + jnp.dot(xb_ref[...].astype(BF16), wb_ref[...], preferred_element_type=F32))
    x = DN_ALPHA * h_ref[...] + mix
    mu = jnp.mean(x, axis=-1, keepdims=True)
    xc = x - mu
    var = jnp.mean(xc * xc, axis=-1, keepdims=True)
    y = xc * lax.rsqrt(var + EPS) * g_ref[...] + b_ref[...]
    o_ref[...] = y
    lg_ref[...] = _dot_hi(y, rw_ref[...]) + rb_ref[...]


def _out_ln_router(xa, xb, w_out, h, ln_g, ln_b, router_w, router_b, tm=256):
    n, d = h.shape
    ka, kb = xa.shape[1], xb.shape[1]
    tm = min(tm, n)
    rw = jnp.pad(router_w, ((0, 0), (0, ROUTER_PAD - N_EXPERTS)))
    rb = jnp.pad(router_b, (0, ROUTER_PAD - N_EXPERTS)).reshape(1, ROUTER_PAD)
    row = lambda w: pl.BlockSpec((tm, w), lambda i: (i, 0))
    const = lambda r, w: pl.BlockSpec((r, w), lambda i: (0, 0))
    return pl.pallas_call(
        _out_ln_router_kernel, grid=(n // tm,),
        in_specs=[row(ka), row(kb), const(ka, d), const(kb, d), row(d), const(1, d), const(1, d),
                  const(d, ROUTER_PAD), const(1, ROUTER_PAD)],
        out_specs=[row(d), row(ROUTER_PAD)],
        out_shape=[jax.ShapeDtypeStruct((n, d), F32), jax.ShapeDtypeStruct((n, ROUTER_PAD), F32)],
        compiler_params=_cparams(("parallel",)), name="out_ln_router",
    )(xa, xb, w_out[:ka].astype(BF16), w_out[ka:].astype(BF16), h, ln_g.reshape(1, d), ln_b.reshape(1, d), rw, rb)


def _combine_ln_kernel(*refs):
    ya_refs = refs[:TOP_K]
    gt_ref, h_ref, g_ref, b_ref, o_ref = refs[TOP_K:]
    gates = gt_ref[...]
    f = gates[:, 0:1] * ya_refs[0][...]
    for kk in range(1, TOP_K):
        f = f + gates[:, kk:kk + 1] * ya_refs[kk][...]
    x = DN_ALPHA * h_ref[...] + f
    mu = jnp.mean(x, axis=-1, keepdims=True)
    xc = x - mu
    var = jnp.mean(xc * xc, axis=-1, keepdims=True)
    o_ref[...] = xc * lax.rsqrt(var + EPS) * g_ref[...] + b_ref[...]


def _combine_ln(y_assign, gates, h, ln_g, ln_b, tm=256):
    n, d = h.shape
    tm = min(tm, n)
    nblk = n // tm
    row = lambda w: pl.BlockSpec((tm, w), lambda i: (i, 0))
    const = lambda r, w: pl.BlockSpec((r, w), lambda i: (0, 0))
    choice = lambda kk: pl.BlockSpec((tm, d), lambda i: (kk * nblk + i, 0))
    return pl.pallas_call(
        _combine_ln_kernel, grid=(nblk,),
        in_specs=[choice(kk) for kk in range(TOP_K)] + [row(TOP_K), row(d), const(1, d), const(1, d)],
        out_specs=row(d), out_shape=jax.ShapeDtypeStruct((n, d), F32),
        compiler_params=_cparams(("parallel",)), name="combine_ln",
    )(*([y_assign] * TOP_K), gates, h, ln_g.reshape(1, d), ln_b.reshape(1, d))


def _moe_kernel(be_ref, nu_ref, x_ref, wgu_ref, bgu_ref, wd_ref, bd_ref, o_ref, wgu_bf, wd_bf):
    i = pl.program_id(0)
    e = be_ref[i]
    e_prev = be_ref[jnp.maximum(i - 1, 0)]

    @pl.when((i == 0) | (e != e_prev))
    def _():
        wgu_bf[...] = wgu_ref[0].astype(BF16)
        wd_bf[...] = wd_ref[0].astype(BF16)

    @pl.when(i < nu_ref[0])
    def _():
        h = jnp.dot(x_ref[...].astype(BF16), wgu_bf[...], preferred_element_type=F32) + bgu_ref[0]
        gate = jnp.minimum(h[:, :D_FF], SWIGLU_LIMIT)
        up = jnp.clip(h[:, D_FF:], -SWIGLU_LIMIT, SWIGLU_LIMIT)
        act = (up + 1.0) * gate * jax.nn.sigmoid(SWIGLU_ALPHA * gate)
        o_ref[...] = jnp.dot(act.astype(BF16), wd_bf[...], preferred_element_type=F32) + bd_ref[0]

    @pl.when(i >= nu_ref[0])
    def _():
        o_ref[...] = jnp.zeros(o_ref.shape, F32)


def _moe_experts(x_sorted, block_e, n_used, w_gu, b_gu, w_down, b_down, blk):
    p_rows, d = x_sorted.shape
    n_blocks = p_rows // blk
    grid_spec = pltpu.PrefetchScalarGridSpec(
        num_scalar_prefetch=2,
        grid=(n_blocks,),
        in_specs=[
            pl.BlockSpec((blk, d), lambda i, be, nu: (i, 0)),
            pl.BlockSpec((1, d, 2 * D_FF), lambda i, be, nu: (be[i], 0, 0)),
            pl.BlockSpec((1, 1, 2 * D_FF), lambda i, be, nu: (be[i], 0, 0)),
            pl.BlockSpec((1, D_FF, d), lambda i, be, nu: (be[i], 0, 0)),
            pl.BlockSpec((1, 1, d), lambda i, be, nu: (be[i], 0, 0)),
        ],
        out_specs=pl.BlockSpec((blk, d), lambda i, be, nu: (i, 0)),
        scratch_shapes=[pltpu.VMEM((d, 2 * D_FF), BF16), pltpu.VMEM((D_FF, d), BF16)],
    )
    return pl.pallas_call(
        _moe_kernel, grid_spec=grid_spec, out_shape=jax.ShapeDtypeStruct((p_rows, d), F32),
        compiler_params=_cparams(("arbitrary",)), name="moe_experts",
    )(block_e, n_used, x_sorted, w_gu, b_gu.reshape(N_EXPERTS, 1, 2 * D_FF), w_down,
      b_down.reshape(N_EXPERTS, 1, d))


def _moe(xt, logits, w_gu, b_gu, w_down, b_down, ln_g, ln_b):
    n, d = xt.shape
    blk = 256 if n >= 8192 else 128
    top_val, top_idx = lax.top_k(logits, TOP_K)
    gates = jax.nn.softmax(top_val, axis=-1)
    nk = n * TOP_K
    flat_e = top_idx.reshape(-1).astype(jnp.int32)
    onehot = (flat_e[:, None] == jnp.arange(N_EXPERTS, dtype=jnp.int32)[None, :]).astype(jnp.int32)
    csum = jnp.cumsum(onehot, axis=0)
    counts = csum[-1]
    rank = jnp.sum((csum - onehot) * onehot, axis=1)
    padded = (counts + blk - 1) // blk * blk
    pend = jnp.cumsum(padded)
    pstart = pend - padded
    dest = pstart[flat_e] + rank
    n_blocks = -(-(nk + N_EXPERTS * (blk - 1)) // blk)
    p_rows = n_blocks * blk
    buf_tok = jnp.zeros((p_rows,), jnp.int32).at[dest].set(jnp.arange(nk, dtype=jnp.int32) // TOP_K,
                                                            unique_indices=True)
    block_e = jnp.minimum(jnp.searchsorted(pend, jnp.arange(n_blocks, dtype=jnp.int32) * blk, side='right'),
                          N_EXPERTS - 1).astype(jnp.int32)
    n_used = (pend[-1] // blk).astype(jnp.int32).reshape(1)
    x_sorted = xt[buf_tok]
    y_buf = _moe_experts(x_sorted, block_e, n_used, w_gu, b_gu, w_down, b_down, blk)
    y_assign = y_buf[dest.reshape(n, TOP_K).T.reshape(-1)]
    return _combine_ln(y_assign, gates, xt, ln_g, ln_b)


def _rms_norm(x, g):
    return x * lax.rsqrt(jnp.mean(x * x, axis=-1, keepdims=True) + EPS) * g


def _rope(x, pos, theta, n_rot):
    half = n_rot // 2
    inv = jnp.exp(-math.log(theta) * jnp.arange(half, dtype=F32) * (2.0 / n_rot))
    ang = pos[:, None] * inv[None, :]
    cos = jnp.cos(ang)[:, None, :]
    sin = jnp.sin(ang)[:, None, :]
    x1 = x[..., :half]
    x2 = x[..., half:n_rot]
    return jnp.concatenate([x1 * cos - x2 * sin, x2 * cos + x1 * sin, x[..., n_rot:]], axis=-1)


def _even_mixer(x, pos, h0_re, h0_im, sample_ctx, lam_init, w_in, lam_re, lam_im, log_dt, b_re, b_im,
                c_re, c_im, d_skip, w_glu, lq1, lk1, lq2, lk2, subln, w_out):
    bsz, L, _ = x.shape
    n = bsz * L
    proj = _mm(x.reshape(n, D_MODEL), w_in.astype(BF16))
    batch_major = L % 8 == 0
    q = proj[:, A_WIDTH:A_WIDTH + B_QK]
    k = proj[:, A_WIDTH + B_QK:A_WIDTH + 2 * B_QK]
    v = proj[:, A_WIDTH + 2 * B_QK:]
    s5w = _s5_weights(lam_re, lam_im, log_dt, b_re, b_im, c_re, c_im)
    proj3 = proj.reshape(bsz, L, proj.shape[1])
    u_in = proj3 if batch_major else proj3[:, :, :A_WIDTH].transpose(1, 0, 2)
    y_s5, h_re, h_im = _s5_glu(u_in, h0_re.reshape(bsz, S5_STATES), h0_im.reshape(bsz, S5_STATES), s5w,
                               d_skip, w_glu.astype(BF16), steps=min(L, 16), batch_major=batch_major)
    y_a = (y_s5 if batch_major else y_s5.transpose(1, 0, 2)).reshape(n, A_WIDTH)
    h_re = h_re.reshape(bsz, A_GROUPS, A_STATE)
    h_im = h_im.reshape(bsz, A_GROUPS, A_STATE)
    q = _rope(q.reshape(bsz, L, 2 * B_HEADS, B_DK), pos, ROPE_THETA, B_ROT)
    k = _rope(k.reshape(bsz, L, 2 * B_HEADS, B_DK), pos, ROPE_THETA, B_ROT)
    lam = jnp.exp(jnp.sum(lq1 * lk1)) - jnp.exp(jnp.sum(lq2 * lk2)) + lam_init
    if sample_ctx is None:
        y_b = _diff_attn_prompt(q.reshape(n, B_QK), k.reshape(n, B_QK), v, lam, subln, bsz, L, lam_init)
    else:
        cache_k, cache_v, layer, page_table = sample_ctx
        y_b = _diff_attn_paged(q.reshape(bsz, L, B_HEADS, 2, B_DK), k.reshape(bsz, L, B_HEADS, 2, B_DK),
                               v.reshape(bsz, L, B_HEADS, B_DV), cache_k, cache_v, layer, page_table, lam, subln,
                               lam_init).reshape(n, B_HEADS * B_DV)
    return ((y_a, y_b, w_out), h_re, h_im, k.reshape(bsz, L, B_HEADS, 2 * B_DK),
            v.reshape(bsz, L, B_HEADS, B_DV))


def _odd_mixer(x, pos, conv0, s0, sample_ctx, w_in, conv_w, a_log, dt_bias, gdn_g,
               q_norm, w_uq, kv_norm, w_uk, w_uv, w_out):
    bsz, L, _ = x.shape
    n = bsz * L
    zcols = lambda w: jnp.zeros((D_MODEL, w), F32)
    ba_end = C_CONV_CH + C_V + 2 * C_HEADS
    w_in_p = jnp.concatenate([w_in[:, :ba_end], zcols(GDN_BA_COLS - 2 * C_HEADS), w_in[:, ba_end:],
                              zcols(128 - D_ROPE)], axis=1).astype(BF16)
    o_cq = C_CONV_CH + C_V + GDN_BA_COLS
    o_ckv = o_cq + D_Q_LORA
    o_kr = o_ckv + D_KV_LORA
    proj = _mm(x.reshape(n, D_MODEL), w_in_p)
    proj3 = proj.reshape(bsz, L, proj.shape[1])
    if L >= C_CHUNK:
        tm, tmo, chunk, nchunk, zlen = 512, 512, C_CHUNK, 512 // C_CHUNK, C_CHUNK
    else:
        tm, tmo, chunk, nchunk, zlen = L, 8, 8, 1, L
    gq, gk, gv, gbg = _gdn_pre(proj3, conv0, conv_w, a_log, dt_bias, tm, tmo)
    y_c, S = _gdn_chunks(gq, gk, gv, gbg, proj3, s0, gdn_g, chunk, nchunk, zlen)
    conv_new = proj3[:, L - (C_CONV - 1):, :C_CONV_CH]
    c_q = proj3[..., o_cq:o_ckv]
    c_kv = proj3[..., o_ckv:o_kr]
    k_r = proj3[..., o_kr:o_kr + D_ROPE]
    qf = _mm(_rms_norm(c_q, q_norm).reshape(n, D_Q_LORA), w_uq.astype(BF16))
    qf = qf.reshape(bsz, L, D_HEADS, D_NOPE + D_ROPE)
    q_rope = _rope(qf[..., D_NOPE:], pos, MLA_ROPE_THETA, D_ROPE)
    q_nope = qf[..., :D_NOPE].reshape(n, D_HEADS * D_NOPE)
    w_bd = jnp.zeros((D_HEADS * D_NOPE, D_HEADS * D_KV_LORA), F32)
    for h in range(D_HEADS):
        w_bd = w_bd.at[h * D_NOPE:(h + 1) * D_NOPE, h * D_KV_LORA:(h + 1) * D_KV_LORA].set(w_uk[h])
    q_lat = _mm(q_nope, w_bd.astype(BF16))
    c = _rms_norm(c_kv, kv_norm)
    kr = _rope(k_r[:, :, None, :], pos, MLA_ROPE_THETA, D_ROPE)[:, :, 0, :]
    if sample_ctx is None:
        y_d = _mla_attn_prompt(q_lat, q_rope.reshape(n, D_HEADS * D_ROPE), c.reshape(n, D_KV_LORA),
                               kr.reshape(n, D_ROPE), w_uv.astype(BF16), bsz, L)
    else:
        cache_lat, cache_rope, layer, page_table = sample_ctx
        o_lat = _mla_attn_paged(q_lat.reshape(bsz, L, D_HEADS, D_KV_LORA), q_rope, c, kr, cache_lat, cache_rope,
                                layer, page_table)
        w_uv_bd = jnp.zeros((D_HEADS * D_KV_LORA, D_HEADS * D_DV), F32)
        for h in range(D_HEADS):
            w_uv_bd = w_uv_bd.at[h * D_KV_LORA:(h + 1) * D_KV_LORA, h * D_DV:(h + 1) * D_DV].set(w_uv[h])
        y_d = _mm(o_lat.reshape(n, D_HEADS * D_KV_LORA), w_uv_bd.astype(BF16))
    return (y_c.reshape(n, C_V), y_d, w_out), S, conv_new, c, kr


def _post_block(h, mix, li, router_w, router_b, moe_w_gu, moe_b_gu, moe_w_down, moe_b_down, ln_g, ln_b):
    bsz, L, d = h.shape
    xa, xb, w_out = mix
    h1, logits = _out_ln_router(xa, xb, w_out, h.reshape(-1, d), ln_g[li, 0], ln_b[li, 0], router_w[li],
                                router_b[li])
    h2 = _moe(h1, logits[:, :N_EXPERTS], moe_w_gu[li], moe_b_gu[li], moe_w_down[li], moe_b_down[li],
              ln_g[li, 1], ln_b[li, 1])
    return h2.reshape(bsz, L, d)


def kernel(x_prompt, x_sample, state_a_re, state_a_im, cache_b_k, cache_b_v, state_c, state_c_conv,
           cache_d_latent, cache_d_rope, page_table, w_in_even, s5_lam_re, s5_lam_im, s5_log_dt,
           s5_b_re, s5_b_im, s5_c_re, s5_c_im, s5_d, s5_w_glu, diff_lq1, diff_lk1, diff_lq2, diff_lk2,
           diff_subln, w_out_even, w_in_odd, gdn_conv_w, gdn_a_log, gdn_dt_bias, gdn_norm, mla_q_norm,
           mla_w_uq, mla_kv_norm, mla_w_uk, mla_w_uv, w_out_odd, router_w, router_b, moe_w_gu, moe_b_gu,
           moe_w_down, moe_b_down, ln_g, ln_b):
    bp, lp, _ = x_prompt.shape
    bs, ls, _ = x_sample.shape
    pos_p = jnp.arange(lp, dtype=F32)
    pos_s = PAST_LEN + jnp.arange(ls, dtype=F32)
    hp, hs = x_prompt, x_sample
    outs_p = {k: [] for k in ('a_re', 'a_im', 'b_k', 'b_v', 'c', 'c_conv', 'd_lat', 'd_rope')}
    outs_s = {k: [] for k in outs_p}
    for li in range(DEPTH):
        if li % 2 == 0:
            e = li // 2
            lam_init = 0.8 - 0.6 * math.exp(-0.3 * li)
            ew = (w_in_even[e], s5_lam_re[e], s5_lam_im[e], s5_log_dt[e], s5_b_re[e], s5_b_im[e],
                  s5_c_re[e], s5_c_im[e], s5_d[e], s5_w_glu[e], diff_lq1[e], diff_lk1[e], diff_lq2[e],
                  diff_lk2[e], diff_subln[e], w_out_even[e])
            z0 = jnp.zeros((bp, A_GROUPS, A_STATE), F32)
            mp, hr, hi, kk, vv = _even_mixer(hp, pos_p, z0, z0, None, lam_init, *ew)
            for key, val in zip(('a_re', 'a_im', 'b_k', 'b_v'), (hr, hi, kk, vv)):
                outs_p[key].append(val)
            ctx = (cache_b_k, cache_b_v, e, page_table)
            ms, hr, hi, kk, vv = _even_mixer(hs, pos_s, state_a_re[e], state_a_im[e], ctx, lam_init, *ew)
            for key, val in zip(('a_re', 'a_im', 'b_k', 'b_v'), (hr, hi, kk, vv)):
                outs_s[key].append(val)
        else:
            o = li // 2
            ow = (w_in_odd[o], gdn_conv_w[o], gdn_a_log[o], gdn_dt_bias[o], gdn_norm[o], mla_q_norm[o],
                  mla_w_uq[o], mla_kv_norm[o], mla_w_uk[o], mla_w_uv[o], w_out_odd[o])
            conv0 = jnp.zeros((bp, C_CONV - 1, C_CONV_CH), F32)
            s0 = jnp.zeros((bp, C_HEADS, C_DK, C_DV), F32)
            mp, S, cb, cl, kr = _odd_mixer(hp, pos_p, conv0, s0, None, *ow)
            for key, val in zip(('c', 'c_conv', 'd_lat', 'd_rope'), (S, cb, cl, kr)):
                outs_p[key].append(val)
            ctx = (cache_d_latent, cache_d_rope, o, page_table)
            ms, S, cb, cl, kr = _odd_mixer(hs, pos_s, state_c_conv[o], state_c[o], ctx, *ow)
            for key, val in zip(('c', 'c_conv', 'd_lat', 'd_rope'), (S, cb, cl, kr)):
                outs_s[key].append(val)
        post = (li, router_w, router_b, moe_w_gu, moe_b_gu, moe_w_down, moe_b_down, ln_g, ln_b)
        hp = _post_block(hp, mp, *post)
        hs = _post_block(hs, ms, *post)
    keys = ('a_re', 'a_im', 'b_k', 'b_v', 'c', 'c_conv', 'd_lat', 'd_rope')
    return ((hp, hs) + tuple(jnp.stack(outs_p[k]) for k in keys) + tuple(jnp.stack(outs_s[k]) for k in keys))
```
